```python
import math
import jax, jax.numpy as jnp
from jax import lax
import numpy as np

D_MODEL = 2048
BATCH = 16
SEQ = 256
DEPTH = 4
DEC_BATCH = 4
DEC_SEQ = 4096
PAST_LEN = 256

GRID_W = 64
D_MIX = D_MODEL
D_A = D_MIX // 2
HEAD_A = 64
N_HEADS_A = D_A // HEAD_A
D_B = D_MIX - D_A
N_GROUPS_B = 8
HEAD_B = D_B // N_GROUPS_B
CHUNK = 128
DECAY_LORA = 64
A_LORA = 64
GATE_LORA = 160
C_SHIFT = 3 * D_A + DECAY_LORA + A_LORA + GATE_LORA
P_IN = C_SHIFT + 2 * D_B
D_FF = -(-8 * D_MODEL // (3 * 256)) * 256
RMS_EPS = 1e-6
GN_EPS = HEAD_A * 1e-5
LN_EPS = 1e-5
F32 = jnp.float32

kernel_name = 'hymba_rwkv7_gmlp_flow_step'


def _rmsnorm(x, g):
    xf = x.astype(F32)
    y = xf * lax.rsqrt(jnp.mean(xf * xf, axis=-1, keepdims=True) + RMS_EPS)
    return (y * g).astype(x.dtype)


def _shift_context(p, mu):
    prev = jnp.pad(p[:, :-1], ((0, 0), (1, 0), (0, 0)))
    nxt = jnp.pad(p[:, 1:], ((0, 0), (0, 1), (0, 0)))
    return p + mu[0] * (prev - p) + mu[1] * (nxt - p)


def _shift_grid(p, mu):
    B, T, C = p.shape
    rows = T // GRID_W
    g = p.reshape(B, rows, GRID_W, C)
    left = jnp.pad(g[:, :, :-1], ((0, 0), (0, 0), (1, 0), (0, 0)))
    right = jnp.pad(g[:, :, 1:], ((0, 0), (0, 0), (0, 1), (0, 0)))
    up = jnp.pad(g[:, :-1], ((0, 0), (1, 0), (0, 0), (0, 0)))
    down = jnp.pad(g[:, 1:], ((0, 0), (0, 1), (0, 0), (0, 0)))
    out = g + mu[0] * (left - g) + mu[1] * (right - g) + mu[2] * (up - g) + mu[3] * (down - g)
    return out.reshape(B, T, C)


def _delta_scan(r, w, k, v, aa, bb, s0, reverse):
    xs = tuple(jnp.moveaxis(t, 1, 0) for t in (r, w, k, v, aa, bb))

    def step(S, inp):
        r_t, w_t, k_t, v_t, a_t, b_t = inp
        sa = jnp.einsum('bhvk,bhk->bhv', S, a_t)
        S = S * w_t[:, :, None, :] + sa[..., None] * b_t[:, :, None, :] + v_t[..., None] * k_t[:, :, None, :]
        return S, jnp.einsum('bhvk,bhk->bhv', S, r_t)

    s_fin, ys = lax.scan(step, s0, xs, reverse=reverse)
    return jnp.moveaxis(ys, 0, 1), s_fin


def _group_norm(y, w, b):
    mean = jnp.mean(y, axis=-1, keepdims=True)
    yc = y - mean
    var = jnp.mean(yc * yc, axis=-1, keepdims=True)
    return yc * lax.rsqrt(var + GN_EPS) * w + b


def _rwkv7_bidir(p, s0, lp):
    dtype = p.dtype
    p = p.astype(F32)
    B, T, _ = p.shape
    r, k, v, wd, ad, gd = jnp.split(
        p, [D_A, 2 * D_A, 3 * D_A, 3 * D_A + DECAY_LORA, 3 * D_A + DECAY_LORA + A_LORA], axis=-1)
    heads = lambda t: t.reshape(B, T, N_HEADS_A, HEAD_A)
    kk = heads(k * lp['k_k'])
    kk = kk * lax.rsqrt(jnp.maximum(jnp.sum(kk * kk, axis=-1, keepdims=True), 1e-24))
    gate = jax.nn.sigmoid(gd) @ lp['g2']
    rh, vh = heads(r), heads(v)
    wt = jnp.tanh(wd)
    out = jnp.zeros((B, T, N_HEADS_A, HEAD_A), F32)
    finals = []
    for d in range(2):
        w_log = -jax.nn.softplus(-(lp['w0'][d] + wt @ lp['w2'][d])) - 0.5
        decay = heads(jnp.exp(-jnp.exp(w_log)))
        a = jax.nn.sigmoid(lp['a0'][d] + ad @ lp['a2'][d])
        kd = heads(k * (1.0 + (a - 1.0) * lp['k_a']))
        y, s_fin = _delta_scan(rh, decay, kd, vh, -kk, kk * heads(a),
                               s0[:, d].astype(F32), reverse=(d == 1))
        out = out + _group_norm(y, lp['gn_w'], lp['gn_b']) \
            + jnp.sum(rh * kd * lp['r_k'], axis=-1, keepdims=True) * vh
        finals.append(s_fin)
    o = (out.reshape(B, T, D_A) * gate).astype(dtype)
    return o, jnp.stack(finals, axis=1)


def _chunk_gmlp(p, lp):
    dtype = p.dtype
    B, T, _ = p.shape
    u, v = jnp.split(p, 2, axis=-1)
    v = v.astype(F32).reshape(B, T // CHUNK, CHUNK, N_GROUPS_B, HEAD_B)
    mean = jnp.mean(v, axis=-1, keepdims=True)
    vc = v - mean
    v = vc * lax.rsqrt(jnp.mean(vc * vc, axis=-1, keepdims=True) + LN_EPS) * lp['gmlp_ln_g'] + lp['gmlp_ln_b']
    s = jnp.einsum('gij,bnjgd->bnigd', lp['w_spatial'], v) + lp['b_spatial'].T[:, :, None]
    return (u * s.reshape(B, T, D_B)).astype(dtype)


def _layer(x, mod, s0, shift_fn, lp):
    sh1, sc1, g1, sh2, sc2, g2 = jnp.split(mod, 6, axis=-1)
    h = _rmsnorm(x, lp['norm1_g']) * (1.0 + sc1) + sh1
    p = h @ lp['w_in']
    o_a, s_fin = _rwkv7_bidir(shift_fn(p[..., :C_SHIFT], lp['mu_shift']), s0, lp)
    o_b = _chunk_gmlp(jax.nn.gelu(p[..., C_SHIFT:]), lp)
    x = x + g1 * (jnp.concatenate([o_a, o_b], axis=-1) @ lp['w_out'])
    h = _rmsnorm(x, lp['norm2_g']) * (1.0 + sc2) + sh2
    ff = (jax.nn.silu(h @ lp['w_ffn_gate']) * (h @ lp['w_ffn_up'])) @ lp['w_ffn_down']
    return x + g2 * ff, s_fin


def setup_inputs(seed: int = 0) -> dict:
    key = jax.random.key(seed)
    ks = jax.random.split(key, 32)
    nrm = lambda k, shape, s: jax.random.normal(k, shape, F32) * s
    L = DEPTH
    return {
        'x_prompt': nrm(ks[0], (BATCH, SEQ, D_MODEL), 1.0),
        'x_sample': nrm(ks[1], (DEC_BATCH, DEC_SEQ, D_MODEL), 1.0),
        'state_rwkv': nrm(ks[2], (DEC_BATCH, DEPTH, 2, N_HEADS_A, HEAD_A, HEAD_A), 1.0),
        'c': nrm(ks[3], (DEC_BATCH, D_MODEL), 1.0),
        'c_ctx': nrm(ks[4], (D_MODEL,), 1.0),
        'w_mod': nrm(ks[5], (L, D_MODEL, 6 * D_MODEL), 0.5 * D_MODEL ** -0.5),
        'b_mod': nrm(ks[6], (L, 6 * D_MODEL), 0.02),
        'norm1_g': 1.0 + nrm(ks[7], (L, D_MODEL), 0.05),
        'w_in': nrm(ks[8], (L, D_MODEL, P_IN), D_MODEL ** -0.5),
        'mu_shift': jax.random.uniform(ks[9], (L, 4, C_SHIFT), F32, 0.0, 0.5),
        'w0': jax.random.uniform(ks[10], (L, 2, D_A), F32, -4.0, 1.0),
        'w2': nrm(ks[11], (L, 2, DECAY_LORA, D_A), 0.5 * DECAY_LORA ** -0.5),
        'a0': nrm(ks[12], (L, 2, D_A), 0.5),
        'a2': nrm(ks[13], (L, 2, A_LORA, D_A), 0.5 * A_LORA ** -0.5),
        'g2': nrm(ks[14], (L, GATE_LORA, D_A), GATE_LORA ** -0.5),
        'k_k': 0.85 + nrm(ks[15], (L, D_A), 0.05),
        'k_a': 1.0 + nrm(ks[16], (L, D_A), 0.05),
        'r_k': nrm(ks[17], (L, N_HEADS_A, HEAD_A), 0.1),
        'gn_w': 1.0 + nrm(ks[18], (L, N_HEADS_A, HEAD_A), 0.05),
        'gn_b': nrm(ks[19], (L, N_HEADS_A, HEAD_A), 0.02),
        'gmlp_ln_g': 1.0 + nrm(ks[20], (L, N_GROUPS_B, HEAD_B), 0.05),
        'gmlp_ln_b': nrm(ks[21], (L, N_GROUPS_B, HEAD_B), 0.02),
        'w_spatial': nrm(ks[22], (L, N_GROUPS_B, CHUNK, CHUNK), CHUNK ** -0.5),
        'b_spatial': 1.0 + nrm(ks[23], (L, N_GROUPS_B, CHUNK), 0.05),
        'w_out': nrm(ks[24], (L, D_MIX, D_MODEL), D_MIX ** -0.5),
        'norm2_g': 1.0 + nrm(ks[25], (L, D_MODEL), 0.05),
        'w_ffn_gate': nrm(ks[26], (L, D_MODEL, D_FF), D_MODEL ** -0.5),
        'w_ffn_up': nrm(ks[27], (L, D_MODEL, D_FF), D_MODEL ** -0.5),
        'w_ffn_down': nrm(ks[28], (L, D_FF, D_MODEL), D_FF ** -0.5),
        'final_norm_g': 1.0 + nrm(ks[29], (D_MODEL,), 0.05),
    }


def reference(x_prompt, x_sample, state_rwkv, c, c_ctx, w_mod, b_mod, norm1_g, w_in, mu_shift,
              w0, w2, a0, a2, g2, k_k, k_a, r_k, gn_w, gn_b, gmlp_ln_g, gmlp_ln_b, w_spatial,
              b_spatial, w_out, norm2_g, w_ffn_gate, w_ffn_up, w_ffn_down, final_norm_g):
    x_ctx = x_prompt
    x_lat = x_sample
    s0_ctx = jnp.zeros((x_prompt.shape[0], 2, N_HEADS_A, HEAD_A, HEAD_A), F32)
    ctx_states = []
    for l in range(DEPTH):
        lp = {
            'norm1_g': norm1_g[l], 'w_in': w_in[l], 'mu_shift': mu_shift[l],
            'w0': w0[l], 'w2': w2[l], 'a0': a0[l], 'a2': a2[l], 'g2': g2[l],
            'k_k': k_k[l], 'k_a': k_a[l], 'r_k': r_k[l], 'gn_w': gn_w[l], 'gn_b': gn_b[l],
            'gmlp_ln_g': gmlp_ln_g[l], 'gmlp_ln_b': gmlp_ln_b[l],
            'w_spatial': w_spatial[l], 'b_spatial': b_spatial[l], 'w_out': w_out[l],
            'norm2_g': norm2_g[l], 'w_ffn_gate': w_ffn_gate[l], 'w_ffn_up': w_ffn_up[l],
            'w_ffn_down': w_ffn_down[l],
        }
        mod_ctx = (jax.nn.silu(c_ctx) @ w_mod[l] + b_mod[l])[None, None, :]
        mod_lat = (jax.nn.silu(c) @ w_mod[l] + b_mod[l])[:, None, :]
        x_ctx, s_ctx = _layer(x_ctx, mod_ctx, s0_ctx, _shift_context, lp)
        ctx_states.append(s_ctx)
        x_lat, _ = _layer(x_lat, mod_lat, state_rwkv[:, l], _shift_grid, lp)
    new_state_rwkv = jnp.stack(ctx_states, axis=1).astype(x_prompt.dtype)
    y_prompt = _rmsnorm(x_ctx, final_norm_g)
    y_sample = _rmsnorm(x_lat, final_norm_g)
    return (y_prompt, y_sample, new_state_rwkv)
```

```python
import functools
import math

import jax
import jax.numpy as jnp
from jax import lax
from jax.experimental import pallas as pl
from jax.experimental.pallas import tpu as pltpu

F32 = jnp.float32
BF16 = jnp.bfloat16

D_MODEL = 2048
DEPTH = 4
N_CTX_SEQ = 16
CTX_LEN = 256
N_LAT = 4
LAT_LEN = 4096
GRID_W = 64
D_A = 1024
HEAD = 64
N_HEADS = 16
D_B = 1024
N_GROUPS_B = 8
CHUNK = 128
DECAY_LORA = 64
A_LORA = 64
GATE_LORA = 160
D_FF = 5632
RMS_EPS = 1e-6
GN_EPS = HEAD * 1e-5
LN_EPS = 1e-5

GROUP_ROWS = 4096
N_ROWS = N_CTX_SEQ * CTX_LEN + N_LAT * LAT_LEN
CTX_ROWS = N_CTX_SEQ * CTX_LEN
LANES = 128
SUBLANES = 8

COL_R, COL_K, COL_V = 0, D_A, 2 * D_A
COL_LORA = 3 * D_A
LORA_W = 512
COL_U = COL_LORA + LORA_W
COL_G = COL_U + D_B
P_PAD = COL_G + D_B

VMEM_LIMIT = 48 * 1024 * 1024


def _cparams(sem):
    return pltpu.CompilerParams(dimension_semantics=sem, vmem_limit_bytes=VMEM_LIMIT)


def _silu(x):
    return x * jax.nn.sigmoid(x)


def _gelu_tanh(x):
    return 0.5 * x * (1.0 + jnp.tanh(math.sqrt(2.0 / math.pi) * (x + 0.044715 * (x * x * x))))


def _modnorm(x, g, scale, shift):
    y = x * lax.rsqrt(jnp.mean(x * x, axis=-1, keepdims=True) + RMS_EPS)
    return (y * g) * (1.0 + scale) + shift


def _mod_kernel(c_ref, w_ref, b_ref, o_ref):
    a = _silu(c_ref[...]).astype(BF16)
    o_ref[0] = jnp.dot(a, w_ref[0].astype(BF16), preferred_element_type=F32) + b_ref[0]


def _modulation(c_all, w_mod, b_mod):
    tn = 1024
    n6 = 6 * D_MODEL
    return pl.pallas_call(
        _mod_kernel,
        grid=(DEPTH, n6 // tn),
        in_specs=[
            pl.BlockSpec((SUBLANES, D_MODEL), lambda l, j: (0, 0)),
            pl.BlockSpec((1, D_MODEL, tn), lambda l, j: (l, 0, j)),
            pl.BlockSpec((1, 1, tn), lambda l, j: (l, 0, j)),
        ],
        out_specs=pl.BlockSpec((1, SUBLANES, tn), lambda l, j: (l, 0, j)),
        out_shape=jax.ShapeDtypeStruct((DEPTH, SUBLANES, n6), F32),
        compiler_params=_cparams(("arbitrary", "arbitrary")),
        name="modulation",
    )(c_all, w_mod, b_mod.reshape(DEPTH, 1, n6))


def _inproj_kernel(x_ref, mod_ref, g_ref, w_ref, o_ref, h_ref):
    @pl.when(pl.program_id(1) == 0)
    def _():
        m = mod_ref[0]
        h_ref[...] = _modnorm(x_ref[...], g_ref[...], m[1:2], m[0:1]).astype(BF16)

    o_ref[...] = jnp.dot(h_ref[...], w_ref[...], preferred_element_type=F32)


def _inproj(x, mod_l, g, w):
    tm, tn = 1024, 512
    per_group = GROUP_ROWS // tm
    return pl.pallas_call(
        _inproj_kernel,
        grid=(N_ROWS // tm, P_PAD // tn),
        in_specs=[
            pl.BlockSpec((tm, D_MODEL), lambda i, j: (i, 0)),
            pl.BlockSpec((1, 6, D_MODEL), lambda i, j: (i // per_group, 0, 0)),
            pl.BlockSpec((1, D_MODEL), lambda i, j: (0, 0)),
            pl.BlockSpec((D_MODEL, tn), lambda i, j: (0, j)),
        ],
        out_specs=pl.BlockSpec((tm, tn), lambda i, j: (i, j)),
        out_shape=jax.ShapeDtypeStruct((N_ROWS, P_PAD), F32),
        scratch_shapes=[pltpu.VMEM((tm, D_MODEL), BF16)],
        compiler_params=_cparams(("arbitrary", "arbitrary")),
        name="inproj",
    )(x, mod_l, g, w)


PREP_TM = 512
PREP_TILES_PER_GROUP = GROUP_ROWS // PREP_TM
HALO = GRID_W


def _shift(x, top, bot, mu, row, is_ctx, has_top, has_bot):
    tm = x.shape[0]
    period = jnp.where(is_ctx, CTX_LEN, GRID_W)
    pos = row & (period - 1)
    prev = jnp.where(pos == 0, 0.0, pltpu.roll(x, 1, axis=0))
    nxt = jnp.where(pos == period - 1, 0.0, pltpu.roll(x, tm - 1, axis=0))
    out = x + mu[0:1] * (prev - x) + mu[1:2] * (nxt - x)
    up = jnp.concatenate([jnp.where(has_top, top, 0.0), x[: tm - HALO]], axis=0)
    down = jnp.concatenate([x[HALO:], jnp.where(has_bot, bot, 0.0)], axis=0)
    grid_terms = mu[2:3] * (up - x) + mu[3:4] * (down - x)
    return out + jnp.where(is_ctx, 0.0, grid_terms)


def _seg_sum(x, bd):
    hi = x.astype(BF16)
    lo = (x - hi.astype(F32)).astype(BF16)
    return jnp.dot(hi, bd, preferred_element_type=F32) + jnp.dot(lo, bd, preferred_element_type=F32)


def _softplus(x):
    return jnp.maximum(x, 0.0) + jnp.log1p(jnp.exp(-jnp.abs(x)))


def _prep_kernel(r_ref, rt_ref, rb_ref, k_ref, kt_ref, kb_ref, v_ref, vt_ref, vb_ref,
                 l_ref, lt_ref, lb_ref, mur_ref, muk_ref, muv_ref, mul_ref,
                 w0_ref, a0_ref, w2_ref, a2_ref, g2_ref, kk_ref, ka_ref, bd_ref,
                 ro_ref, vo_ref, nkk_ref, dec0_ref, dec1_ref, b0_ref, b1_ref,
                 kd0_ref, kd1_ref, gate_ref):
    i = pl.program_id(0)
    is_ctx = i < PREP_TILES_PER_GROUP
    has_top = (i % PREP_TILES_PER_GROUP) != 0
    has_bot = (i % PREP_TILES_PER_GROUP) != PREP_TILES_PER_GROUP - 1
    row = lax.broadcasted_iota(jnp.int32, (PREP_TM, 1), 0)
    sh = functools.partial(_shift, row=row, is_ctx=is_ctx, has_top=has_top, has_bot=has_bot)

    r = sh(r_ref[...], rt_ref[...], rb_ref[...], mur_ref[...])
    k = sh(k_ref[...], kt_ref[...], kb_ref[...], muk_ref[...])
    v = sh(v_ref[...], vt_ref[...], vb_ref[...], muv_ref[...])
    lo = sh(l_ref[...], lt_ref[...], lb_ref[...], mul_ref[...])
    wt = jnp.tanh(lo[:, 0:LANES]).astype(BF16)
    ad = lo[:, LANES:2 * LANES].astype(BF16)
    gs = jax.nn.sigmoid(lo[:, 2 * LANES:]).astype(BF16)

    kk = k * kk_ref[...]
    kk = kk * lax.rsqrt(jnp.maximum(_seg_sum(kk * kk, bd_ref[...]), 1e-24))
    ro_ref[...] = r
    vo_ref[...] = v
    nkk_ref[...] = -kk
    gate_ref[...] = jnp.dot(gs, g2_ref[...], preferred_element_type=F32)
    ka = ka_ref[...]
    for d, (dec_ref, b_ref, kd_ref) in enumerate(((dec0_ref, b0_ref, kd0_ref), (dec1_ref, b1_ref, kd1_ref))):
        wl = w0_ref[d:d + 1] + jnp.dot(wt, w2_ref[d], preferred_element_type=F32)
        w_log = -_softplus(-wl) - 0.5
        dec_ref[...] = jnp.exp(-jnp.exp(w_log))
        a = jax.nn.sigmoid(a0_ref[d:d + 1] + jnp.dot(ad, a2_ref[d], preferred_element_type=F32))
        kd_ref[...] = k * (1.0 + (a - 1.0) * ka)
        b_ref[...] = kk * a


def _prep(p, mu, w0, a0, w2, a2, g2, k_k, k_a, bd):
    tm = PREP_TM
    nt = N_ROWS // tm
    hb = tm // HALO
    n_halo = N_ROWS // HALO

    def main(col0, width):
        cb = col0 // width
        return pl.BlockSpec((tm, width), lambda i, j: (i, cb + (j if width == LANES else 0)))

    def top(col0, width):
        cb = col0 // width
        return pl.BlockSpec((HALO, width),
                            lambda i, j: (jnp.maximum(i * hb - 1, 0), cb + (j if width == LANES else 0)))

    def bot(col0, width):
        cb = col0 // width
        return pl.BlockSpec((HALO, width),
                            lambda i, j: (jnp.minimum(i * hb + hb, n_halo - 1), cb + (j if width == LANES else 0)))

    def colp(rows, col0):
        cb = col0 // LANES
        return pl.BlockSpec((rows, LANES), lambda i, j: (0, cb + j))

    in_specs = []
    for col0 in (COL_R, COL_K, COL_V):
        in_specs += [main(col0, LANES), top(col0, LANES), bot(col0, LANES)]
    in_specs += [main(COL_LORA, LORA_W), top(COL_LORA, LORA_W), bot(COL_LORA, LORA_W)]
    in_specs += [colp(4, COL_R), colp(4, COL_K), colp(4, COL_V),
                 pl.BlockSpec((4, LORA_W), lambda i, j: (0, COL_LORA // LORA_W))]
    in_specs += [
        colp(2, 0), colp(2, 0),
        pl.BlockSpec((2, LANES, LANES), lambda i, j: (0, 0, j)),
        pl.BlockSpec((2, LANES, LANES), lambda i, j: (0, 0, j)),
        pl.BlockSpec((2 * LANES, LANES), lambda i, j: (0, j)),
        colp(1, 0), colp(1, 0),
        pl.BlockSpec((LANES, LANES), lambda i, j: (0, 0)),
    ]
    out_spec = pl.BlockSpec((tm, LANES), lambda i, j: (i, j))
    out_shape = jax.ShapeDtypeStruct((N_ROWS, D_A), F32)
    return pl.pallas_call(
        _prep_kernel,
        grid=(nt, D_A // LANES),
        in_specs=in_specs,
        out_specs=[out_spec] * 10,
        out_shape=[out_shape] * 10,
        compiler_params=_cparams(("arbitrary", "arbitrary")),
        name="rwkv_prep",
    )(p, p, p, p, p, p, p, p, p, p, p, p, mu, mu, mu, mu, w0, a0, w2, a2, g2, k_k, k_a, bd)


SCAN_TT = 32


def _scan_kernel(a_ref, w_ref, b_ref, kd_ref, r_ref, vv_ref, s0_ref, gnw_ref, gnb_ref, rk_ref,
                 z_ref, sfin_ref, s_ref):
    @pl.when(pl.program_id(1) == 0)
    def _():
        s_ref[...] = s0_ref[0]

    gnw = gnw_ref[...]
    gnb = gnb_ref[...]
    rk = rk_ref[...]

    def step(i, carry):
        def row(ref, k):
            return ref[0, i, pl.ds(k, 1), :]

        sa = s_ref[0] * row(a_ref, 0)
        for k in range(1, HEAD):
            sa = sa + s_ref[k] * row(a_ref, k)
        vv = vv_ref[0, i]
        y = None
        for k in range(HEAD):
            s = s_ref[k] * row(w_ref, k) + sa * row(b_ref, k) + vv * row(kd_ref, k)
            s_ref[k] = s
            t = s * row(r_ref, k)
            y = t if y is None else y + t
        mean = jnp.sum(jnp.sum(y, axis=0), axis=0, keepdims=True) * (1.0 / HEAD)
        yc = y - mean
        var = jnp.sum(jnp.sum(yc * yc, axis=0), axis=0, keepdims=True) * (1.0 / HEAD)
        z = yc * lax.rsqrt(var + GN_EPS) * gnw + gnb
        rfull = r_ref[0, i].reshape(SUBLANES, SUBLANES, LANES)
        kdfull = kd_ref[0, i].reshape(SUBLANES, SUBLANES, LANES)
        bonus = jnp.sum(jnp.sum(rfull * kdfull * rk, axis=0), axis=0, keepdims=True)
        z_ref[0, i] = z + bonus * vv
        return carry

    lax.fori_loop(0, SCAN_TT, step, 0)

    @pl.when(pl.program_id(1) == pl.num_programs(1) - 1)
    def _():
        sfin_ref[0] = s_ref[...]


def _scan(a, w, b, kd, r, vv, s0, gnw, gnb, rk):
    g, t = a.shape[0], a.shape[1]
    tt = SCAN_TT
    kspec = pl.BlockSpec((1, tt, HEAD, LANES), lambda gi, ti: (gi, ti, 0, 0))
    vspec = pl.BlockSpec((1, tt, SUBLANES, SUBLANES, LANES), lambda gi, ti: (gi, ti, 0, 0, 0))
    sspec = pl.BlockSpec((1, HEAD, SUBLANES, SUBLANES, LANES), lambda gi, ti: (gi, 0, 0, 0, 0))
    pspec = pl.BlockSpec((SUBLANES, SUBLANES, LANES), lambda gi, ti: (0, 0, 0))
    return pl.pallas_call(
        _scan_kernel,
        grid=(g, t // tt),
        in_specs=[kspec] * 5 + [vspec, sspec, pspec, pspec, pspec],
        out_specs=[vspec, sspec],
        out_shape=[jax.ShapeDtypeStruct((g, t, SUBLANES, SUBLANES, LANES), F32),
                   jax.ShapeDtypeStruct((g, HEAD, SUBLANES, SUBLANES, LANES), F32)],
        scratch_shapes=[pltpu.VMEM((HEAD, SUBLANES, SUBLANES, LANES), F32)],
        compiler_params=_cparams(("arbitrary", "arbitrary")),
        name="delta_scan",
    )(a, w, b, kd, r, vv, s0, gnw, gnb, rk)


def _gmlp_kernel(u_ref, v_ref, lng_ref, lnb_ref, ws_ref, bs_ref, o_ref):
    g = pl.program_id(1)
    v = _gelu_tanh(v_ref[...])
    mean = jnp.mean(v, axis=-1, keepdims=True)
    vc = v - mean
    var = jnp.mean(vc * vc, axis=-1, keepdims=True)
    vn = (vc * lax.rsqrt(var + LN_EPS) * lng_ref[pl.ds(g, 1), :] + lnb_ref[pl.ds(g, 1), :]).astype(BF16)
    u = _gelu_tanh(u_ref[...])
    ws = ws_ref[0]
    bs = bs_ref[0]
    for c in range(u.shape[0] // CHUNK):
        rows = slice(c * CHUNK, (c + 1) * CHUNK)
        s = jnp.dot(ws, vn[rows], preferred_element_type=F32) + bs
        o_ref[rows, :] = (u[rows] * s).astype(o_ref.dtype)


def _gmlp(p, ln_g, ln_b, w_sp, b_sp):
    tm = 512
    ub, gb = COL_U // LANES, COL_G // LANES
    return pl.pallas_call(
        _gmlp_kernel,
        grid=(N_ROWS // tm, N_GROUPS_B),
        in_specs=[
            pl.BlockSpec((tm, LANES), lambda i, g: (i, ub + g)),
            pl.BlockSpec((tm, LANES), lambda i, g: (i, gb + g)),
            pl.BlockSpec((N_GROUPS_B, LANES), lambda i, g: (0, 0)),
            pl.BlockSpec((N_GROUPS_B, LANES), lambda i, g: (0, 0)),
            pl.BlockSpec((1, CHUNK, CHUNK), lambda i, g: (g, 0, 0)),
            pl.BlockSpec((1, CHUNK, LANES), lambda i, g: (g, 0, 0)),
        ],
        out_specs=pl.BlockSpec((tm, LANES), lambda i, g: (i, g)),
        out_shape=jax.ShapeDtypeStruct((N_ROWS, D_B), BF16),
        compiler_params=_cparams(("arbitrary", "arbitrary")),
        name="chunk_gmlp",
    )(p, p, ln_g, ln_b, w_sp, b_sp)


def _outproj_kernel(x_ref, z0_ref, z1_ref, gate_ref, ob_ref, mod_ref, w_ref, o_ref, a_ref):
    @pl.when(pl.program_id(1) == 0)
    def _():
        a_ref[:, :D_A] = ((z0_ref[...] + z1_ref[...]) * gate_ref[...]).astype(BF16)
        a_ref[:, D_A:] = ob_ref[...]

    tn = o_ref.shape[1]
    j = pl.program_id(1)
    g1 = mod_ref[0, 2:3, pl.ds(pl.multiple_of(j * tn, tn), tn)]
    o_ref[...] = x_ref[...] + g1 * jnp.dot(a_ref[...], w_ref[...], preferred_element_type=F32)


def _outproj(x, z0, z1, gate, ob, mod_l, w):
    tm, tn = 512, 1024
    per_group = GROUP_ROWS // tm
    half = pl.BlockSpec((tm, D_A), lambda i, j: (i, 0))
    return pl.pallas_call(
        _outproj_kernel,
        grid=(N_ROWS // tm, D_MODEL // tn),
        in_specs=[
            pl.BlockSpec((tm, tn), lambda i, j: (i, j)),
            half, half, half, half,
            pl.BlockSpec((1, 6, D_MODEL), lambda i, j: (i // per_group, 0, 0)),
            pl.BlockSpec((D_MODEL, tn), lambda i, j: (0, j)),
        ],
        out_specs=pl.BlockSpec((tm, tn), lambda i, j: (i, j)),
        out_shape=jax.ShapeDtypeStruct((N_ROWS, D_MODEL), F32),
        scratch_shapes=[pltpu.VMEM((tm, D_MODEL), BF16)],
        compiler_params=_cparams(("arbitrary", "arbitrary")),
        name="outproj",
    )(x, z0, z1, gate, ob, mod_l, w)


def _ffn_kernel(x_ref, mod_ref, g_ref, wg_ref, wu_ref, wd_ref, o_ref, h_ref, acc_ref):
    f = pl.program_id(1)

    @pl.when(f == 0)
    def _():
        m = mod_ref[0]
        h_ref[...] = _modnorm(x_ref[...], g_ref[...], m[4:5], m[3:4]).astype(BF16)
        acc_ref[...] = jnp.zeros_like(acc_ref)

    h = h_ref[...]
    gate = jnp.dot(h, wg_ref[...], preferred_element_type=F32)
    up = jnp.dot(h, wu_ref[...], preferred_element_type=F32)
    act = (_silu(gate) * up).astype(BF16)
    acc_ref[...] += jnp.dot(act, wd_ref[...], preferred_element_type=F32)

    @pl.when(f == pl.num_programs(1) - 1)
    def _():
        o_ref[...] = x_ref[...] + mod_ref[0, 5:6] * acc_ref[...]


def _ffn(x, mod_l, g, wg, wu, wd):
    tm, tf = 512, 512
    per_group = GROUP_ROWS // tm
    return pl.pallas_call(
        _ffn_kernel,
        grid=(N_ROWS // tm, D_FF // tf),
        in_specs=[
            pl.BlockSpec((tm, D_MODEL), lambda i, f: (i, 0)),
            pl.BlockSpec((1, 6, D_MODEL), lambda i, f: (i // per_group, 0, 0)),
            pl.BlockSpec((1, D_MODEL), lambda i, f: (0, 0)),
            pl.BlockSpec((D_MODEL, tf), lambda i, f: (0, f)),
            pl.BlockSpec((D_MODEL, tf), lambda i, f: (0, f)),
            pl.BlockSpec((tf, D_MODEL), lambda i, f: (f, 0)),
        ],
        out_specs=pl.BlockSpec((tm, D_MODEL), lambda i, f: (i, 0)),
        out_shape=jax.ShapeDtypeStruct((N_ROWS, D_MODEL), F32),
        scratch_shapes=[pltpu.VMEM((tm, D_MODEL), BF16), pltpu.VMEM((tm, D_MODEL), F32)],
        compiler_params=_cparams(("arbitrary", "arbitrary")),
        name="ffn",
    )(x, mod_l, g, wg, wu, wd)


def _final_norm_kernel(x_ref, g_ref, o_ref):
    x = x_ref[...]
    o_ref[...] = x * lax.rsqrt(jnp.mean(x * x, axis=-1, keepdims=True) + RMS_EPS) * g_ref[...]


def _final_norm(x, g):
    tm = 512
    return pl.pallas_call(
        _final_norm_kernel,
        grid=(N_ROWS // tm,),
        in_specs=[pl.BlockSpec((tm, D_MODEL), lambda i: (i, 0)),
                  pl.BlockSpec((1, D_MODEL), lambda i: (0, 0))],
        out_specs=pl.BlockSpec((tm, D_MODEL), lambda i: (i, 0)),
        out_shape=jax.ShapeDtypeStruct((N_ROWS, D_MODEL), F32),
        compiler_params=_cparams(("arbitrary",)),
        name="final_norm",
    )(x, g)


def _to_chain(x0, x1):
    def one(x0p, x1p, g, t):
        f = x0p.reshape(g, 4, t, N_HEADS, HEAD).transpose(0, 2, 4, 1, 3)
        b = jnp.flip(x1p.reshape(g, 4, t, N_HEADS, HEAD), axis=2).transpose(0, 2, 4, 1, 3)
        return jnp.stack([f, b], axis=3).reshape(g, t, HEAD, LANES)

    ctx = one(x0[:CTX_ROWS], x1[:CTX_ROWS], N_CTX_SEQ // 4, CTX_LEN)
    lat = one(x0[CTX_ROWS:], x1[CTX_ROWS:], 1, LAT_LEN)
    return ctx, lat


def _from_chain(z_ctx, z_lat):
    def one(z, g, t):
        z = z.reshape(g, t, HEAD, 2, 4, N_HEADS)
        f = z[:, :, :, 0].transpose(0, 3, 1, 4, 2).reshape(g * 4 * t, D_A)
        b = jnp.flip(z[:, :, :, 1], axis=1).transpose(0, 3, 1, 4, 2).reshape(g * 4 * t, D_A)
        return f, b

    f0, b0 = one(z_ctx, N_CTX_SEQ // 4, CTX_LEN)
    f1, b1 = one(z_lat, 1, LAT_LEN)
    return jnp.concatenate([f0, f1], axis=0), jnp.concatenate([b0, b1], axis=0)


def _head_param_tile(p):
    return jnp.tile(p.T, (1, LANES // N_HEADS)).reshape(SUBLANES, SUBLANES, LANES)


def _pad_rows(w, rows):
    return jnp.pad(w, [(0, 0)] * (w.ndim - 2) + [(0, rows - w.shape[-2]), (0, 0)])


def kernel(x_prompt, x_sample, state_rwkv, c, c_ctx, w_mod, b_mod, norm1_g, w_in, mu_shift, w0, w2, a0, a2,
           g2, k_k, k_a, r_k, gn_w, gn_b, gmlp_ln_g, gmlp_ln_b, w_spatial, b_spatial, w_out, norm2_g,
           w_ffn_gate, w_ffn_up, w_ffn_down, final_norm_g):
    x = jnp.concatenate([x_prompt.reshape(CTX_ROWS, D_MODEL), x_sample.reshape(N_LAT * LAT_LEN, D_MODEL)], axis=0)
    c_all = jnp.concatenate([c_ctx[None], c, jnp.zeros((SUBLANES - 1 - N_LAT, D_MODEL), F32)], axis=0)
    mod = _modulation(c_all, w_mod, b_mod).reshape(DEPTH, SUBLANES, 6, D_MODEL)

    lane = jnp.arange(LANES)
    bd = (lane[:, None] // HEAD == lane[None, :] // HEAD).astype(BF16)

    def pad_cols(cols):
        rkv, wd, ad, gd, ug = (cols[..., :3 * D_A], cols[..., 3 * D_A:3 * D_A + DECAY_LORA],
                               cols[..., 3 * D_A + DECAY_LORA:3 * D_A + DECAY_LORA + A_LORA],
                               cols[..., 3 * D_A + DECAY_LORA + A_LORA:3 * D_A + DECAY_LORA + A_LORA + GATE_LORA],
                               cols[..., 3 * D_A + DECAY_LORA + A_LORA + GATE_LORA:])
        z = lambda n: jnp.zeros(cols.shape[:-1] + (n,), cols.dtype)
        return jnp.concatenate([rkv, wd, z(LANES - DECAY_LORA), ad, z(LANES - A_LORA),
                                gd, z(2 * LANES - GATE_LORA), ug], axis=-1)

    ctx_states = []
    for l in range(DEPTH):
        mod_l = mod[l]
        w_in_l = pad_cols(w_in[l]).astype(BF16)
        mu_l = pad_cols(mu_shift[l])[:, :COL_U]
        p = _inproj(x, mod_l, norm1_g[l][None], w_in_l)

        outs = _prep(p, mu_l, w0[l], a0[l],
                     _pad_rows(w2[l], LANES).astype(BF16), _pad_rows(a2[l], LANES).astype(BF16),
                     _pad_rows(g2[l], 2 * LANES).astype(BF16), k_k[l][None], k_a[l][None], bd)
        r, v, nkk, dec0, dec1, b0, b1, kd0, kd1, gate = outs

        a_c, a_l = _to_chain(nkk, nkk)
        w_c, w_l = _to_chain(dec0, dec1)
        b_c, b_l = _to_chain(b0, b1)
        kd_c, kd_l = _to_chain(kd0, kd1)
        r_c, r_l = _to_chain(r, r)
        v_c, v_l = _to_chain(v, v)
        shp5 = lambda t: t.reshape(t.shape[0], t.shape[1], SUBLANES, SUBLANES, LANES)
        gnw_t, gnb_t, rk_t = _head_param_tile(gn_w[l]), _head_param_tile(gn_b[l]), _head_param_tile(r_k[l])

        s0_ctx = jnp.zeros((N_CTX_SEQ // 4, HEAD, SUBLANES, SUBLANES, LANES), F32)
        s0_lat = state_rwkv[:, l].transpose(4, 3, 1, 0, 2).reshape(1, HEAD, SUBLANES, SUBLANES, LANES)
        z_c, sfin = _scan(a_c, w_c, b_c, kd_c, r_c, shp5(v_c), s0_ctx, gnw_t, gnb_t, rk_t)
        z_l, _ = _scan(a_l, w_l, b_l, kd_l, r_l, shp5(v_l), s0_lat, gnw_t, gnb_t, rk_t)
        ctx_states.append(
            sfin.reshape(N_CTX_SEQ // 4, HEAD, HEAD, 2, 4, N_HEADS).transpose(0, 4, 3, 5, 2, 1)
            .reshape(N_CTX_SEQ, 2, N_HEADS, HEAD, HEAD))
        z0, z1 = _from_chain(z_c.reshape(N_CTX_SEQ // 4, CTX_LEN, HEAD, LANES),
                             z_l.reshape(1, LAT_LEN, HEAD, LANES))

        ob = _gmlp(p, gmlp_ln_g[l], gmlp_ln_b[l], w_spatial[l].astype(BF16),
                   jnp.broadcast_to(b_spatial[l][:, :, None], (N_GROUPS_B, CHUNK, LANES)))
        x = _outproj(x, z0, z1, gate, ob, mod_l, w_out[l].astype(BF16))
        x = _ffn(x, mod_l, norm2_g[l][None], w_ffn_gate[l].astype(BF16), w_ffn_up[l].astype(BF16),
                 w_ffn_down[l].astype(BF16))

    y = _final_norm(x, final_norm_g[None])
    y_prompt = y[:CTX_ROWS].reshape(N_CTX_SEQ, CTX_LEN, D_MODEL)
    y_sample = y[CTX_ROWS:].reshape(N_LAT, LAT_LEN, D_MODEL)
    new_state = jnp.stack(ctx_states, axis=1)
    return (y_prompt, y_sample, new_state)
```

```python
import functools
import math

import jax
import jax.numpy as jnp
from jax import lax
from jax.experimental import pallas as pl
from jax.experimental.pallas import tpu as pltpu

F32 = jnp.float32
BF16 = jnp.bfloat16

D_MODEL = 2048
DEPTH = 4
N_CTX_SEQ = 16
CTX_LEN = 256
N_LAT = 4
LAT_LEN = 4096
GRID_W = 64
D_A = 1024
HEAD = 64
N_HEADS = 16
D_B = 1024
N_GROUPS_B = 8
CHUNK = 128
DECAY_LORA = 64
A_LORA = 64
GATE_LORA = 160
D_FF = 5632
RMS_EPS = 1e-6
GN_EPS = HEAD * 1e-5
LN_EPS = 1e-5

GROUP_ROWS = 4096
CTX_ROWS = N_CTX_SEQ * CTX_LEN
N_ROWS = CTX_ROWS + N_LAT * LAT_LEN
LANES = 128
SUBLANES = 8
SEQ_PER_GROUP = 4
HALF = HEAD // 2

COL_R, COL_K, COL_V = 0, D_A, 2 * D_A
COL_LORA = 3 * D_A
LORA_W = 512
COL_U = COL_LORA + LORA_W
COL_G = COL_U + D_B
P_PAD = COL_G + D_B

VMEM_LIMIT = 48 * 1024 * 1024

CTX_STREAM = (0, N_CTX_SEQ, CTX_LEN)
LAT_STREAM = (CTX_ROWS, N_LAT, LAT_LEN)


def _cparams(sem):
    return pltpu.CompilerParams(dimension_semantics=sem, vmem_limit_bytes=VMEM_LIMIT)


def _silu(x):
    return x * jax.nn.sigmoid(x)


def _gelu_tanh(x):
    return 0.5 * x * (1.0 + jnp.tanh(math.sqrt(2.0 / math.pi) * (x + 0.044715 * (x * x * x))))


def _modnorm(x, g, scale, shift):
    y = x * lax.rsqrt(jnp.mean(x * x, axis=-1, keepdims=True) + RMS_EPS)
    return (y * g) * (1.0 + scale) + shift


def _mod_kernel(c_ref, w_ref, b_ref, o_ref):
    a = _silu(c_ref[...]).astype(BF16)
    o_ref[0] = jnp.dot(a, w_ref[0].astype(BF16), preferred_element_type=F32) + b_ref[0]


def _modulation(c_all, w_mod, b_mod):
    tn = 1024
    n6 = 6 * D_MODEL
    return pl.pallas_call(
        _mod_kernel,
        grid=(DEPTH, n6 // tn),
        in_specs=[
            pl.BlockSpec((SUBLANES, D_MODEL), lambda l, j: (0, 0)),
            pl.BlockSpec((1, D_MODEL, tn), lambda l, j: (l, 0, j)),
            pl.BlockSpec((1, 1, tn), lambda l, j: (l, 0, j)),
        ],
        out_specs=pl.BlockSpec((1, SUBLANES, tn), lambda l, j: (l, 0, j)),
        out_shape=jax.ShapeDtypeStruct((DEPTH, SUBLANES, n6), F32),
        compiler_params=_cparams(("arbitrary", "arbitrary")),
        name="modulation",
    )(c_all, w_mod, b_mod.reshape(DEPTH, 1, n6))


def _inproj_kernel(x_ref, mod_ref, g_ref, w_ref, o_ref, h_ref):
    @pl.when(pl.program_id(1) == 0)
    def _():
        m = mod_ref[0]
        h_ref[...] = _modnorm(x_ref[...], g_ref[...], m[1:2], m[0:1]).astype(BF16)

    o_ref[...] = jnp.dot(h_ref[...], w_ref[...], preferred_element_type=F32)


def _inproj(x, mod_l, g, w):
    tm, tn = 1024, 512
    per_group = GROUP_ROWS // tm
    return pl.pallas_call(
        _inproj_kernel,
        grid=(N_ROWS // tm, P_PAD // tn),
        in_specs=[
            pl.BlockSpec((tm, D_MODEL), lambda i, j: (i, 0)),
            pl.BlockSpec((1, 6, D_MODEL), lambda i, j: (i // per_group, 0, 0)),
            pl.BlockSpec((1, D_MODEL), lambda i, j: (0, 0)),
            pl.BlockSpec((D_MODEL, tn), lambda i, j: (0, j)),
        ],
        out_specs=pl.BlockSpec((tm, tn), lambda i, j: (i, j)),
        out_shape=jax.ShapeDtypeStruct((N_ROWS, P_PAD), F32),
        scratch_shapes=[pltpu.VMEM((tm, D_MODEL), BF16)],
        compiler_params=_cparams(("arbitrary", "arbitrary")),
        name="inproj",
    )(x, mod_l, g, w)


PREP_TM = 256
HALO = GRID_W


def _shift(x, top, bot, mu, row, is_ctx, has_top, has_bot):
    tm = x.shape[0]
    period = jnp.where(is_ctx, CTX_LEN, GRID_W)
    pos = row & (period - 1)
    local = lax.broadcasted_iota(jnp.int32, (tm, 1), 0)
    prev = jnp.where(local == 0, top[HALO - 1:HALO], pltpu.roll(x, 1, axis=0))
    nxt = jnp.where(local == tm - 1, bot[0:1], pltpu.roll(x, tm - 1, axis=0))
    prev = jnp.where(pos == 0, 0.0, prev)
    nxt = jnp.where(pos == period - 1, 0.0, nxt)
    out = x + mu[0:1] * (prev - x) + mu[1:2] * (nxt - x)
    up = jnp.concatenate([jnp.where(has_top, top, 0.0), x[: tm - HALO]], axis=0)
    down = jnp.concatenate([x[HALO:], jnp.where(has_bot, bot, 0.0)], axis=0)
    grid_terms = mu[2:3] * (up - x) + mu[3:4] * (down - x)
    return out + jnp.where(is_ctx, 0.0, grid_terms)


def _head_sum(x):
    s = x[:, 0:LANES]
    for j in range(1, D_A // LANES):
        s = s + x[:, j * LANES:(j + 1) * LANES]
    for sh in (N_HEADS, 2 * N_HEADS, 4 * N_HEADS):
        s = s + pltpu.roll(s, sh, axis=1)
    return s


def _tile_heads(s):
    return jnp.concatenate([s] * (D_A // LANES), axis=1)


def _softplus(x):
    return jnp.maximum(x, 0.0) + jnp.log1p(jnp.exp(-jnp.abs(x)))


def _prep_kernel(r_ref, rt_ref, rb_ref, k_ref, kt_ref, kb_ref, v_ref, vt_ref, vb_ref,
                 l_ref, lt_ref, lb_ref, mu_ref, w0_ref, a0_ref, w2_ref, a2_ref, g2_ref,
                 kk_ref, ka_ref, rk_ref,
                 nkk_o, dec0_o, dec1_o, b0_o, b1_o, kd0_o, kd1_o, r_o, v_o, gate_o, vnat_o, bsum_o,
                 *, ctx_tiles, tiles_per_image):
    i = pl.program_id(0)
    is_ctx = i < ctx_tiles
    img_tile = (i - ctx_tiles) % tiles_per_image
    has_top = img_tile != 0
    has_bot = img_tile != tiles_per_image - 1
    row = i * PREP_TM + lax.broadcasted_iota(jnp.int32, (PREP_TM, 1), 0)
    sh = functools.partial(_shift, row=row, is_ctx=is_ctx, has_top=has_top, has_bot=has_bot)

    r = sh(r_ref[...], rt_ref[...], rb_ref[...], mu_ref[:, COL_R:COL_R + D_A])
    k = sh(k_ref[...], kt_ref[...], kb_ref[...], mu_ref[:, COL_K:COL_K + D_A])
    v = sh(v_ref[...], vt_ref[...], vb_ref[...], mu_ref[:, COL_V:COL_V + D_A])
    lo = sh(l_ref[...], lt_ref[...], lb_ref[...], mu_ref[:, COL_LORA:COL_LORA + LORA_W])
    wt = jnp.tanh(lo[:, 0:LANES]).astype(BF16)
    ad = lo[:, LANES:2 * LANES].astype(BF16)
    gs = jax.nn.sigmoid(lo[:, 2 * LANES:]).astype(BF16)

    kk = k * kk_ref[...]
    inv = lax.rsqrt(jnp.maximum(_head_sum(kk * kk), 1e-24))
    kk = kk * _tile_heads(inv)
    r_o[...] = r.T
    v_o[...] = v.T
    vnat_o[...] = v
    nkk_o[...] = (-kk).T
    gate_o[...] = jnp.dot(gs, g2_ref[...], preferred_element_type=F32)
    ka = ka_ref[...]
    rrk = r * rk_ref[...]
    bsum = None
    for d, (dec_o, b_o, kd_o) in enumerate(((dec0_o, b0_o, kd0_o), (dec1_o, b1_o, kd1_o))):
        wl = w0_ref[d:d + 1] + jnp.dot(wt, w2_ref[d], preferred_element_type=F32)
        w_log = -_softplus(-wl) - 0.5
        dec_o[...] = jnp.exp(-jnp.exp(w_log)).T
        a = jax.nn.sigmoid(a0_ref[d:d + 1] + jnp.dot(ad, a2_ref[d], preferred_element_type=F32))
        kd = k * (1.0 + (a - 1.0) * ka)
        kd_o[...] = kd.T
        b_o[...] = (kk * a).T
        bonus = _head_sum(rrk * kd)
        bsum = bonus if bsum is None else bsum + bonus
    bsum_o[...] = bsum


def _prep(p, mu, w0, a0, w2, a2, g2, k_k, k_a, r_k, *, n_rows, ctx_rows, image_rows):
    tm = PREP_TM
    hb = tm // HALO
    n_halo = n_rows // HALO

    def main(col0, width):
        return pl.BlockSpec((tm, width), lambda i: (i, col0 // width))

    def top(col0, width):
        return pl.BlockSpec((HALO, width), lambda i: (jnp.maximum(i * hb - 1, 0), col0 // width))

    def bot(col0, width):
        return pl.BlockSpec((HALO, width), lambda i: (jnp.minimum(i * hb + hb, n_halo - 1), col0 // width))

    def full(a):
        nd = a.ndim
        return pl.BlockSpec(a.shape, lambda i: (0,) * nd)

    in_specs = []
    for col0 in (COL_R, COL_K, COL_V):
        in_specs += [main(col0, D_A), top(col0, D_A), bot(col0, D_A)]
    in_specs += [main(COL_LORA, LORA_W), top(COL_LORA, LORA_W), bot(COL_LORA, LORA_W)]
    params = (mu, w0, a0, w2, a2, g2, k_k, k_a, r_k)
    in_specs += [full(a) for a in params]
    t_spec = pl.BlockSpec((D_A, tm), lambda i: (0, i))
    n_spec = pl.BlockSpec((tm, D_A), lambda i: (i, 0))
    t_shape = jax.ShapeDtypeStruct((D_A, n_rows), F32)
    n_shape = jax.ShapeDtypeStruct((n_rows, D_A), F32)
    return pl.pallas_call(
        functools.partial(_prep_kernel, ctx_tiles=ctx_rows // tm, tiles_per_image=image_rows // tm),
        grid=(n_rows // tm,),
        in_specs=in_specs,
        out_specs=[t_spec] * 9 + [n_spec, n_spec, pl.BlockSpec((tm, LANES), lambda i: (i, 0))],
        out_shape=[t_shape] * 9 + [n_shape, n_shape, jax.ShapeDtypeStruct((n_rows, LANES), F32)],
        compiler_params=_cparams(("arbitrary",)),
        name="rwkv_prep",
    )(p, p, p, p, p, p, p, p, p, p, p, p, *params)


RELAYOUT_T = 128
K_PER_BLOCK = LANES // N_HEADS


def _seq_col_block(stream, g, b, tb):
    row0, _, t_len = stream
    return (row0 + (g * SEQ_PER_GROUP + b) * t_len) // RELAYOUT_T + tb


def _to_chain_kernel(*refs, n_arrays):
    ins, outs = refs[:SEQ_PER_GROUP * n_arrays], refs[SEQ_PER_GROUP * n_arrays:]
    for a in range(n_arrays):
        blks = [ins[SEQ_PER_GROUP * a + b][...] for b in range(SEQ_PER_GROUP)]
        for kk in range(K_PER_BLOCK):
            slabs = [blk[kk * N_HEADS:(kk + 1) * N_HEADS] for blk in blks]
            outs[a][0, kk] = jnp.concatenate(slabs + slabs, axis=0).T


def _to_chain(arrays, stream):
    _, n_seq, t_len = stream
    groups = n_seq // SEQ_PER_GROUP
    n = len(arrays)
    in_specs, operands = [], []
    for a in arrays:
        for b in range(SEQ_PER_GROUP):
            in_specs.append(pl.BlockSpec(
                (LANES, RELAYOUT_T),
                functools.partial(lambda b, g, tb, kb: (kb, _seq_col_block(stream, g, b, tb)), b)))
            operands.append(a)
    out_spec = pl.BlockSpec((1, K_PER_BLOCK, RELAYOUT_T, LANES), lambda g, tb, kb: (g, kb, tb, 0))
    out_shape = jax.ShapeDtypeStruct((groups, HEAD, t_len, LANES), F32)
    return pl.pallas_call(
        functools.partial(_to_chain_kernel, n_arrays=n),
        grid=(groups, t_len // RELAYOUT_T, HEAD // K_PER_BLOCK),
        in_specs=in_specs,
        out_specs=[out_spec] * n,
        out_shape=[out_shape] * n,
        compiler_params=_cparams(("arbitrary", "arbitrary", "arbitrary")),
        name="to_chain",
    )(*operands)


def _value_to_chain_kernel(v0_ref, v1_ref, v2_ref, v3_ref, o_ref):
    blks = [r[...] for r in (v0_ref, v1_ref, v2_ref, v3_ref)]
    for vl in range(HALF):
        slabs = [blk[(vh * HALF + vl) * N_HEADS:(vh * HALF + vl + 1) * N_HEADS]
                 for vh in range(2) for blk in blks]
        o_ref[0, pl.ds(vl, RELAYOUT_T, stride=HALF), :] = jnp.concatenate(slabs, axis=0).T


def _value_to_chain(v_t, stream):
    _, n_seq, t_len = stream
    groups = n_seq // SEQ_PER_GROUP
    in_specs = [pl.BlockSpec((D_A, RELAYOUT_T),
                             functools.partial(lambda b, g, tb: (0, _seq_col_block(stream, g, b, tb)), b))
                for b in range(SEQ_PER_GROUP)]
    return pl.pallas_call(
        _value_to_chain_kernel,
        grid=(groups, t_len // RELAYOUT_T),
        in_specs=in_specs,
        out_specs=pl.BlockSpec((1, RELAYOUT_T * HALF, LANES), lambda g, tb: (g, tb, 0)),
        out_shape=jax.ShapeDtypeStruct((groups, t_len * HALF, LANES), F32),
        compiler_params=_cparams(("arbitrary", "arbitrary")),
        name="value_to_chain",
    )(v_t, v_t, v_t, v_t)


def _from_chain_kernel(z0_ref, z1_ref, o_ref, t_ref):
    b = pl.program_id(2)

    @pl.when(b == 0)
    def _():
        for vl in range(HALF):
            rows = pl.ds(vl, RELAYOUT_T, stride=HALF)
            t_ref[vl] = (z0_ref[0, rows, :] + z1_ref[0, rows, :]).T

    for vl in range(HALF):
        for vh in range(2):
            src = pl.ds(pl.multiple_of(vh * (LANES // 2) + b * N_HEADS, N_HEADS), N_HEADS)
            o_ref[(vh * HALF + vl) * N_HEADS:(vh * HALF + vl + 1) * N_HEADS, :] = t_ref[vl, src, :]


def _from_chain(z0, z1, stream, n_rows):
    _, n_seq, t_len = stream
    groups = n_seq // SEQ_PER_GROUP
    z_spec = pl.BlockSpec((1, RELAYOUT_T * HALF, LANES), lambda g, tb, b: (g, tb, 0))
    return pl.pallas_call(
        _from_chain_kernel,
        grid=(groups, t_len // RELAYOUT_T, SEQ_PER_GROUP),
        in_specs=[z_spec, z_spec],
        out_specs=pl.BlockSpec((D_A, RELAYOUT_T), lambda g, tb, b: (0, (g * SEQ_PER_GROUP + b) * (t_len // RELAYOUT_T) + tb)),
        out_shape=jax.ShapeDtypeStruct((D_A, n_seq * t_len), F32),
        scratch_shapes=[pltpu.VMEM((HALF, LANES, RELAYOUT_T), F32)],
        compiler_params=_cparams(("arbitrary", "arbitrary", "arbitrary")),
        name="from_chain",
    )(z0, z1)


SCAN_TT = 32
V_TILES = HALF // SUBLANES


def _scan_kernel(a_ref, w_ref, b_ref, kd_ref, r_ref, vv_ref, s0_ref, gnw_ref, gnb_ref,
                 z_ref, sfin_ref, s_ref, *, reverse):
    @pl.when(pl.program_id(1) == 0)
    def _():
        s_ref[...] = s0_ref[0]

    gnw = gnw_ref[...]
    gnb = gnb_ref[...]

    def step(i, carry):
        ii = SCAN_TT - 1 - i if reverse else i

        def row(ref, k):
            return ref[0, k, pl.ds(ii, 1), :]

        sa = s_ref[0] * row(a_ref, 0)
        for k in range(1, HEAD):
            sa = sa + s_ref[k] * row(a_ref, k)
        vrows = pl.ds(pl.multiple_of(ii * HALF, HALF), HALF)
        vv = vv_ref[0, vrows, :].reshape(V_TILES, SUBLANES, LANES)
        y = None
        for k in range(HEAD):
            s = s_ref[k] * row(w_ref, k) + sa * row(b_ref, k) + vv * row(kd_ref, k)
            s_ref[k] = s
            t = s * row(r_ref, k)
            y = t if y is None else y + t

        def head_mean(x):
            part = jnp.sum(jnp.sum(x, axis=0), axis=0, keepdims=True)
            return (part + pltpu.roll(part, LANES // 2, axis=1)) * (1.0 / HEAD)

        yc = y - head_mean(y)
        z = yc * lax.rsqrt(head_mean(yc * yc) + GN_EPS) * gnw + gnb
        z_ref[0, vrows, :] = z.reshape(HALF, LANES)
        return carry

    lax.fori_loop(0, SCAN_TT, step, 0)

    @pl.when(pl.program_id(1) == pl.num_programs(1) - 1)
    def _():
        sfin_ref[0] = s_ref[...]


def _scan(a, w, b, kd, r, vv, s0, gnw, gnb, *, reverse):
    g, t = a.shape[0], a.shape[2]
    tt = SCAN_TT
    nt = t // tt
    tblk = (lambda ti: nt - 1 - ti) if reverse else (lambda ti: ti)
    kspec = pl.BlockSpec((1, HEAD, tt, LANES), lambda gi, ti: (gi, 0, tblk(ti), 0))
    vspec = pl.BlockSpec((1, tt * HALF, LANES), lambda gi, ti: (gi, tblk(ti), 0))
    sspec = pl.BlockSpec((1, HEAD, V_TILES, SUBLANES, LANES), lambda gi, ti: (gi, 0, 0, 0, 0))
    pspec = pl.BlockSpec((V_TILES, SUBLANES, LANES), lambda gi, ti: (0, 0, 0))
    return pl.pallas_call(
        functools.partial(_scan_kernel, reverse=reverse),
        grid=(g, nt),
        in_specs=[kspec] * 5 + [vspec, sspec, pspec, pspec],
        out_specs=[vspec, sspec],
        out_shape=[jax.ShapeDtypeStruct((g, t * HALF, LANES), F32),
                   jax.ShapeDtypeStruct((g, HEAD, V_TILES, SUBLANES, LANES), F32)],
        scratch_shapes=[pltpu.VMEM((HEAD, V_TILES, SUBLANES, LANES), F32)],
        compiler_params=_cparams(("arbitrary", "arbitrary")),
        name="delta_scan_bwd" if reverse else "delta_scan_fwd",
    )(a, w, b, kd, r, vv, s0, gnw, gnb)


def _gmlp_kernel(u_ref, v_ref, lng_ref, lnb_ref, ws_ref, bs_ref, o_ref):
    g = pl.program_id(1)
    v = _gelu_tanh(v_ref[...])
    mean = jnp.mean(v, axis=-1, keepdims=True)
    vc = v - mean
    var = jnp.mean(vc * vc, axis=-1, keepdims=True)
    vn = (vc * lax.rsqrt(var + LN_EPS) * lng_ref[pl.ds(g, 1), :] + lnb_ref[pl.ds(g, 1), :]).astype(BF16)
    u = _gelu_tanh(u_ref[...])
    ws = ws_ref[0]
    bs = bs_ref[0]
    for c in range(u.shape[0] // CHUNK):
        rows = slice(c * CHUNK, (c + 1) * CHUNK)
        s = jnp.dot(ws, vn[rows], preferred_element_type=F32) + bs
        o_ref[rows, :] = (u[rows] * s).astype(o_ref.dtype)


def _gmlp(p, ln_g, ln_b, w_sp, b_sp):
    tm = 512
    ub, gb = COL_U // LANES, COL_G // LANES
    return pl.pallas_call(
        _gmlp_kernel,
        grid=(N_ROWS // tm, N_GROUPS_B),
        in_specs=[
            pl.BlockSpec((tm, LANES), lambda i, g: (i, ub + g)),
            pl.BlockSpec((tm, LANES), lambda i, g: (i, gb + g)),
            pl.BlockSpec((N_GROUPS_B, LANES), lambda i, g: (0, 0)),
            pl.BlockSpec((N_GROUPS_B, LANES), lambda i, g: (0, 0)),
            pl.BlockSpec((1, CHUNK, CHUNK), lambda i, g: (g, 0, 0)),
            pl.BlockSpec((1, CHUNK, LANES), lambda i, g: (g, 0, 0)),
        ],
        out_specs=pl.BlockSpec((tm, LANES), lambda i, g: (i, g)),
        out_shape=jax.ShapeDtypeStruct((N_ROWS, D_B), BF16),
        compiler_params=_cparams(("arbitrary", "arbitrary")),
        name="chunk_gmlp",
    )(p, p, ln_g, ln_b, w_sp, b_sp)


def _outproj_kernel(x_ref, zc_ref, zl_ref, v_ref, bsum_ref, gate_ref, ob_ref, mod_ref, w_ref, o_ref, a_ref,
                    *, ctx_tiles):
    i = pl.program_id(0)

    @pl.when(pl.program_id(1) == 0)
    def _():
        def fill(z_ref):
            oa = (z_ref[...].T + _tile_heads(bsum_ref[...]) * v_ref[...]) * gate_ref[...]
            a_ref[:, :D_A] = oa.astype(BF16)

        pl.when(i < ctx_tiles)(lambda: fill(zc_ref))
        pl.when(i >= ctx_tiles)(lambda: fill(zl_ref))
        a_ref[:, D_A:] = ob_ref[...]

    tn = o_ref.shape[1]
    j = pl.program_id(1)
    g1 = mod_ref[0, 2:3, pl.ds(pl.multiple_of(j * tn, tn), tn)]
    o_ref[...] = x_ref[...] + g1 * jnp.dot(a_ref[...], w_ref[...], preferred_element_type=F32)


def _outproj(x, z_ctx, z_lat, v, bsum, gate, ob, mod_l, w):
    tm, tn = 512, 1024
    per_group = GROUP_ROWS // tm
    ctx_tiles = CTX_ROWS // tm
    n_lat_tiles = (N_ROWS - CTX_ROWS) // tm
    half = pl.BlockSpec((tm, D_A), lambda i, j: (i, 0))
    return pl.pallas_call(
        functools.partial(_outproj_kernel, ctx_tiles=ctx_tiles),
        grid=(N_ROWS // tm, D_MODEL // tn),
        in_specs=[
            pl.BlockSpec((tm, tn), lambda i, j: (i, j)),
            pl.BlockSpec((D_A, tm), lambda i, j: (0, jnp.minimum(i, ctx_tiles - 1))),
            pl.BlockSpec((D_A, tm), lambda i, j: (0, jnp.clip(i - ctx_tiles, 0, n_lat_tiles - 1))),
            half,
            pl.BlockSpec((tm, LANES), lambda i, j: (i, 0)),
            half, half,
            pl.BlockSpec((1, 6, D_MODEL), lambda i, j: (i // per_group, 0, 0)),
            pl.BlockSpec((D_MODEL, tn), lambda i, j: (0, j)),
        ],
        out_specs=pl.BlockSpec((tm, tn), lambda i, j: (i, j)),
        out_shape=jax.ShapeDtypeStruct((N_ROWS, D_MODEL), F32),
        scratch_shapes=[pltpu.VMEM((tm, D_MODEL), BF16)],
        compiler_params=_cparams(("arbitrary", "arbitrary")),
        name="outproj",
    )(x, z_ctx, z_lat, v, bsum, gate, ob, mod_l, w)


def _ffn_kernel(x_ref, mod_ref, g_ref, wg_ref, wu_ref, wd_ref, o_ref, h_ref, acc_ref):
    f = pl.program_id(1)

    @pl.when(f == 0)
    def _():
        m = mod_ref[0]
        h_ref[...] = _modnorm(x_ref[...], g_ref[...], m[4:5], m[3:4]).astype(BF16)
        acc_ref[...] = jnp.zeros_like(acc_ref)

    h = h_ref[...]
    gate = jnp.dot(h, wg_ref[...], preferred_element_type=F32)
    up = jnp.dot(h, wu_ref[...], preferred_element_type=F32)
    act = (_silu(gate) * up).astype(BF16)
    acc_ref[...] += jnp.dot(act, wd_ref[...], preferred_element_type=F32)

    @pl.when(f == pl.num_programs(1) - 1)
    def _():
        o_ref[...] = x_ref[...] + mod_ref[0, 5:6] * acc_ref[...]


def _ffn(x, mod_l, g, wg, wu, wd):
    tm, tf = 512, 512
    per_group = GROUP_ROWS // tm
    return pl.pallas_call(
        _ffn_kernel,
        grid=(N_ROWS // tm, D_FF // tf),
        in_specs=[
            pl.BlockSpec((tm, D_MODEL), lambda i, f: (i, 0)),
            pl.BlockSpec((1, 6, D_MODEL), lambda i, f: (i // per_group, 0, 0)),
            pl.BlockSpec((1, D_MODEL), lambda i, f: (0, 0)),
            pl.BlockSpec((D_MODEL, tf), lambda i, f: (0, f)),
            pl.BlockSpec((D_MODEL, tf), lambda i, f: (0, f)),
            pl.BlockSpec((tf, D_MODEL), lambda i, f: (f, 0)),
        ],
        out_specs=pl.BlockSpec((tm, D_MODEL), lambda i, f: (i, 0)),
        out_shape=jax.ShapeDtypeStruct((N_ROWS, D_MODEL), F32),
        scratch_shapes=[pltpu.VMEM((tm, D_MODEL), BF16), pltpu.VMEM((tm, D_MODEL), F32)],
        compiler_params=_cparams(("arbitrary", "arbitrary")),
        name="ffn",
    )(x, mod_l, g, wg, wu, wd)


def _final_norm_kernel(x_ref, g_ref, o_ref):
    x = x_ref[...]
    o_ref[...] = x * lax.rsqrt(jnp.mean(x * x, axis=-1, keepdims=True) + RMS_EPS) * g_ref[...]


def _final_norm(x, g):
    tm = 512
    return pl.pallas_call(
        _final_norm_kernel,
        grid=(N_ROWS // tm,),
        in_specs=[pl.BlockSpec((tm, D_MODEL), lambda i: (i, 0)),
                  pl.BlockSpec((1, D_MODEL), lambda i: (0, 0))],
        out_specs=pl.BlockSpec((tm, D_MODEL), lambda i: (i, 0)),
        out_shape=jax.ShapeDtypeStruct((N_ROWS, D_MODEL), F32),
        compiler_params=_cparams(("arbitrary",)),
        name="final_norm",
    )(x, g)


def _head_param_tile(p):
    t = p.T.reshape(2, HALF, 1, N_HEADS)
    t = jnp.broadcast_to(t, (2, HALF, SEQ_PER_GROUP, N_HEADS))
    return t.transpose(1, 0, 2, 3).reshape(V_TILES, SUBLANES, LANES)


def _state_to_chain(s):
    g = s.shape[0]
    s = s.reshape(g, SEQ_PER_GROUP, N_HEADS, 2, HALF, HEAD)
    return s.transpose(0, 5, 4, 3, 1, 2).reshape(g, HEAD, V_TILES, SUBLANES, LANES)


def _state_from_chain(s):
    g = s.shape[0]
    s = s.reshape(g, HEAD, HALF, 2, SEQ_PER_GROUP, N_HEADS)
    return s.transpose(0, 4, 5, 3, 2, 1).reshape(g * SEQ_PER_GROUP, N_HEADS, HEAD, HEAD)


def _rwkv_stream(t_arrays, stream, s0_fwd, s0_bwd, gnw_t, gnb_t, n_rows):
    nkk_t, dec0_t, dec1_t, b0_t, b1_t, kd0_t, kd1_t, r_t, v_t = t_arrays
    nkk_c, dec0_c, dec1_c, b0_c, b1_c, kd0_c, kd1_c, r_c = _to_chain(
        (nkk_t, dec0_t, dec1_t, b0_t, b1_t, kd0_t, kd1_t, r_t), stream)
    vv_c = _value_to_chain(v_t, stream)
    z0, sf0 = _scan(nkk_c, dec0_c, b0_c, kd0_c, r_c, vv_c, s0_fwd, gnw_t, gnb_t, reverse=False)
    z1, sf1 = _scan(nkk_c, dec1_c, b1_c, kd1_c, r_c, vv_c, s0_bwd, gnw_t, gnb_t, reverse=True)
    return _from_chain(z0, z1, stream, n_rows), sf0, sf1


def _index_major(n=N_HEADS, m=HEAD):
    return jnp.arange(n * m).reshape(n, m).T.reshape(-1)


def kernel(x_prompt, x_sample, state_rwkv, c, c_ctx, w_mod, b_mod, norm1_g, w_in, mu_shift, w0, w2, a0, a2,
           g2, k_k, k_a, r_k, gn_w, gn_b, gmlp_ln_g, gmlp_ln_b, w_spatial, b_spatial, w_out, norm2_g,
           w_ffn_gate, w_ffn_up, w_ffn_down, final_norm_g):
    x = jnp.concatenate([x_prompt.reshape(CTX_ROWS, D_MODEL), x_sample.reshape(N_LAT * LAT_LEN, D_MODEL)], axis=0)
    c_all = jnp.concatenate([c_ctx[None], c, jnp.zeros((SUBLANES - 1 - N_LAT, D_MODEL), F32)], axis=0)
    mod = _modulation(c_all, w_mod, b_mod).reshape(DEPTH, SUBLANES, 6, D_MODEL)
    perm = _index_major()

    def pad_cols(cols):
        c0 = 3 * D_A
        r, k, v = (cols[..., j * D_A:(j + 1) * D_A][..., perm] for j in range(3))
        wd = cols[..., c0:c0 + DECAY_LORA]
        ad = cols[..., c0 + DECAY_LORA:c0 + DECAY_LORA + A_LORA]
        gd = cols[..., c0 + DECAY_LORA + A_LORA:c0 + DECAY_LORA + A_LORA + GATE_LORA]
        ug = cols[..., c0 + DECAY_LORA + A_LORA + GATE_LORA:]
        z = lambda n: jnp.zeros(cols.shape[:-1] + (n,), cols.dtype)
        return jnp.concatenate([r, k, v, wd, z(LANES - DECAY_LORA), ad, z(LANES - A_LORA),
                                gd, z(2 * LANES - GATE_LORA), ug], axis=-1)

    def pad_rows(w, rows):
        return jnp.pad(w, [(0, 0)] * (w.ndim - 2) + [(0, rows - w.shape[-2]), (0, 0)])

    ctx_groups = N_CTX_SEQ // SEQ_PER_GROUP
    s0_ctx = jnp.zeros((ctx_groups, HEAD, V_TILES, SUBLANES, LANES), F32)
    ctx_states = []
    for l in range(DEPTH):
        mod_l = mod[l]
        p = _inproj(x, mod_l, norm1_g[l][None], pad_cols(w_in[l]).astype(BF16))

        outs = _prep(p, pad_cols(mu_shift[l])[:, :COL_U], w0[l][:, perm], a0[l][:, perm],
                     pad_rows(w2[l][..., perm], LANES).astype(BF16),
                     pad_rows(a2[l][..., perm], LANES).astype(BF16),
                     pad_rows(g2[l][:, perm], 2 * LANES).astype(BF16),
                     k_k[l][perm][None], k_a[l][perm][None], r_k[l].reshape(-1)[perm][None],
                     n_rows=N_ROWS, ctx_rows=CTX_ROWS, image_rows=LAT_LEN)
        t_arrays, (gate, v_nat, bsum) = outs[:9], outs[9:]
        gnw_t, gnb_t = _head_param_tile(gn_w[l]), _head_param_tile(gn_b[l])

        z_ctx, sf0, sf1 = _rwkv_stream(t_arrays, CTX_STREAM, s0_ctx, s0_ctx, gnw_t, gnb_t, N_ROWS)
        s_lat = state_rwkv[:, l]
        z_lat, _, _ = _rwkv_stream(t_arrays, LAT_STREAM, _state_to_chain(s_lat[None, :, 0]),
                                   _state_to_chain(s_lat[None, :, 1]), gnw_t, gnb_t, N_ROWS)
        ctx_states.append(jnp.stack([_state_from_chain(sf0), _state_from_chain(sf1)], axis=1))

        ob = _gmlp(p, gmlp_ln_g[l], gmlp_ln_b[l], w_spatial[l].astype(BF16),
                   jnp.broadcast_to(b_spatial[l][:, :, None], (N_GROUPS_B, CHUNK, LANES)))
        w_out_l = jnp.concatenate([w_out[l][:D_A][perm], w_out[l][D_A:]], axis=0).astype(BF16)
        x = _outproj(x, z_ctx, z_lat, v_nat, bsum, gate, ob, mod_l, w_out_l)
        x = _ffn(x, mod_l, norm2_g[l][None], w_ffn_gate[l].astype(BF16), w_ffn_up[l].astype(BF16),
                 w_ffn_down[l].astype(BF16))

    y = _final_norm(x, final_norm_g[None])
    y_prompt = y[:CTX_ROWS].reshape(N_CTX_SEQ, CTX_LEN, D_MODEL)
    y_sample = y[CTX_ROWS:].reshape(N_LAT, LAT_LEN, D_MODEL)
    new_state = jnp.stack(ctx_states, axis=1)
    return (y_prompt, y_sample, new_state)
```

```python
import functools
import math

import jax
import jax.numpy as jnp
from jax import lax
from jax.experimental import pallas as pl
from jax.experimental.pallas import tpu as pltpu

F32 = jnp.float32
BF16 = jnp.bfloat16

D_MODEL = 2048
DEPTH = 4
N_CTX_SEQ = 16
CTX_LEN = 256
N_LAT = 4
LAT_LEN = 4096
GRID_W = 64
D_A = 1024
HEAD = 64
N_HEADS = 16
D_B = 1024
N_GROUPS_B = 8
CHUNK = 128
DECAY_LORA = 64
A_LORA = 64
GATE_LORA = 160
D_FF = 5632
RMS_EPS = 1e-6
GN_EPS = HEAD * 1e-5
LN_EPS = 1e-5

GROUP_ROWS = 4096
CTX_ROWS = N_CTX_SEQ * CTX_LEN
N_ROWS = CTX_ROWS + N_LAT * LAT_LEN
LANES = 128
SUBLANES = 8
SEQ_PER_GROUP = 4
HALF = HEAD // 2

COL_R, COL_K, COL_V = 0, D_A, 2 * D_A
COL_LORA = 3 * D_A
LORA_W = 512
COL_U = COL_LORA + LORA_W
COL_G = COL_U + D_B
P_PAD = COL_G + D_B

VMEM_LIMIT = 48 * 1024 * 1024

CTX_STREAM = (0, N_CTX_SEQ, CTX_LEN)
LAT_STREAM = (CTX_ROWS, N_LAT, LAT_LEN)


def _cparams(sem):
    return pltpu.CompilerParams(dimension_semantics=sem, vmem_limit_bytes=VMEM_LIMIT)


def _silu(x):
    return x * jax.nn.sigmoid(x)


def _gelu_tanh(x):
    return 0.5 * x * (1.0 + jnp.tanh(math.sqrt(2.0 / math.pi) * (x + 0.044715 * (x * x * x))))


def _modnorm(x, g, scale, shift):
    y = x * lax.rsqrt(jnp.mean(x * x, axis=-1, keepdims=True) + RMS_EPS)
    return (y * g) * (1.0 + scale) + shift


def _mod_kernel(c_ref, w_ref, b_ref, o_ref):
    a = _silu(c_ref[...]).astype(BF16)
    o_ref[0] = jnp.dot(a, w_ref[0].astype(BF16), preferred_element_type=F32) + b_ref[0]


def _modulation(c_all, w_mod, b_mod):
    tn = 1024
    n6 = 6 * D_MODEL
    return pl.pallas_call(
        _mod_kernel,
        grid=(DEPTH, n6 // tn),
        in_specs=[
            pl.BlockSpec((SUBLANES, D_MODEL), lambda l, j: (0, 0)),
            pl.BlockSpec((1, D_MODEL, tn), lambda l, j: (l, 0, j)),
            pl.BlockSpec((1, 1, tn), lambda l, j: (l, 0, j)),
        ],
        out_specs=pl.BlockSpec((1, SUBLANES, tn), lambda l, j: (l, 0, j)),
        out_shape=jax.ShapeDtypeStruct((DEPTH, SUBLANES, n6), F32),
        compiler_params=_cparams(("arbitrary", "arbitrary")),
        name="modulation",
    )(c_all, w_mod, b_mod.reshape(DEPTH, 1, n6))


def _inproj_kernel(x_ref, mod_ref, g_ref, w_ref, o_ref, h_ref):
    @pl.when(pl.program_id(1) == 0)
    def _():
        m = mod_ref[0]
        h_ref[...] = _modnorm(x_ref[...], g_ref[...], m[1:2], m[0:1]).astype(BF16)

    o_ref[...] = jnp.dot(h_ref[...], w_ref[...], preferred_element_type=F32)


def _inproj(x, mod_l, g, w):
    tm, tn = 1024, 512
    per_group = GROUP_ROWS // tm
    return pl.pallas_call(
        _inproj_kernel,
        grid=(N_ROWS // tm, P_PAD // tn),
        in_specs=[
            pl.BlockSpec((tm, D_MODEL), lambda i, j: (i, 0)),
            pl.BlockSpec((1, 6, D_MODEL), lambda i, j: (i // per_group, 0, 0)),
            pl.BlockSpec((1, D_MODEL), lambda i, j: (0, 0)),
            pl.BlockSpec((D_MODEL, tn), lambda i, j: (0, j)),
        ],
        out_specs=pl.BlockSpec((tm, tn), lambda i, j: (i, j)),
        out_shape=jax.ShapeDtypeStruct((N_ROWS, P_PAD), F32),
        scratch_shapes=[pltpu.VMEM((tm, D_MODEL), BF16)],
        compiler_params=_cparams(("arbitrary", "arbitrary")),
        name="inproj",
    )(x, mod_l, g, w)


PREP_TM = 256
HALO = GRID_W


def _shift(x, top, bot, mu, row, is_ctx, has_top, has_bot):
    tm = x.shape[0]
    period = jnp.where(is_ctx, CTX_LEN, GRID_W)
    pos = row & (period - 1)
    local = lax.broadcasted_iota(jnp.int32, (tm, 1), 0)
    prev = jnp.where(local == 0, top[HALO - 1:HALO], pltpu.roll(x, 1, axis=0))
    nxt = jnp.where(local == tm - 1, bot[0:1], pltpu.roll(x, tm - 1, axis=0))
    prev = jnp.where(pos == 0, 0.0, prev)
    nxt = jnp.where(pos == period - 1, 0.0, nxt)
    out = x + mu[0:1] * (prev - x) + mu[1:2] * (nxt - x)
    up = jnp.concatenate([jnp.where(has_top, top, 0.0), x[: tm - HALO]], axis=0)
    down = jnp.concatenate([x[HALO:], jnp.where(has_bot, bot, 0.0)], axis=0)
    grid_terms = mu[2:3] * (up - x) + mu[3:4] * (down - x)
    return out + jnp.where(is_ctx, 0.0, grid_terms)


def _head_sum(x):
    s = x[:, 0:LANES]
    for j in range(1, D_A // LANES):
        s = s + x[:, j * LANES:(j + 1) * LANES]
    for sh in (N_HEADS, 2 * N_HEADS, 4 * N_HEADS):
        s = s + pltpu.roll(s, sh, axis=1)
    return s


def _tile_heads(s):
    return jnp.concatenate([s] * (D_A // LANES), axis=1)


def _softplus(x):
    return jnp.maximum(x, 0.0) + jnp.log1p(jnp.exp(-jnp.abs(x)))


def _prep_kernel(r_ref, rt_ref, rb_ref, k_ref, kt_ref, kb_ref, v_ref, vt_ref, vb_ref,
                 l_ref, lt_ref, lb_ref, mu_ref, w0_ref, a0_ref, w2_ref, a2_ref, g2_ref,
                 kk_ref, ka_ref, rk_ref,
                 nkk_o, dec0_o, dec1_o, b0_o, b1_o, kd0_o, kd1_o, r_o, v_o, gate_o, vnat_o, bsum_o,
                 *, ctx_tiles, tiles_per_image):
    i = pl.program_id(0)
    is_ctx = i < ctx_tiles
    img_tile = (i - ctx_tiles) % tiles_per_image
    has_top = img_tile != 0
    has_bot = img_tile != tiles_per_image - 1
    row = i * PREP_TM + lax.broadcasted_iota(jnp.int32, (PREP_TM, 1), 0)
    sh = functools.partial(_shift, row=row, is_ctx=is_ctx, has_top=has_top, has_bot=has_bot)

    r = sh(r_ref[...], rt_ref[...], rb_ref[...], mu_ref[:, COL_R:COL_R + D_A])
    k = sh(k_ref[...], kt_ref[...], kb_ref[...], mu_ref[:, COL_K:COL_K + D_A])
    v = sh(v_ref[...], vt_ref[...], vb_ref[...], mu_ref[:, COL_V:COL_V + D_A])
    lo = sh(l_ref[...], lt_ref[...], lb_ref[...], mu_ref[:, COL_LORA:COL_LORA + LORA_W])
    wt = jnp.tanh(lo[:, 0:LANES]).astype(BF16)
    ad = lo[:, LANES:2 * LANES].astype(BF16)
    gs = jax.nn.sigmoid(lo[:, 2 * LANES:]).astype(BF16)

    kk = k * kk_ref[...]
    inv = lax.rsqrt(jnp.maximum(_head_sum(kk * kk), 1e-24))
    kk = kk * _tile_heads(inv)
    r_o[...] = r.T
    v_o[...] = v.T
    vnat_o[...] = v
    nkk_o[...] = (-kk).T
    gate_o[...] = jnp.dot(gs, g2_ref[...], preferred_element_type=F32)
    ka = ka_ref[...]
    rrk = r * rk_ref[...]
    bsum = None
    for d, (dec_o, b_o, kd_o) in enumerate(((dec0_o, b0_o, kd0_o), (dec1_o, b1_o, kd1_o))):
        wl = w0_ref[d:d + 1] + jnp.dot(wt, w2_ref[d], preferred_element_type=F32)
        w_log = -_softplus(-wl) - 0.5
        dec_o[...] = jnp.exp(-jnp.exp(w_log)).T
        a = jax.nn.sigmoid(a0_ref[d:d + 1] + jnp.dot(ad, a2_ref[d], preferred_element_type=F32))
        kd = k * (1.0 + (a - 1.0) * ka)
        kd_o[...] = kd.T
        b_o[...] = (kk * a).T
        bonus = _head_sum(rrk * kd)
        bsum = bonus if bsum is None else bsum + bonus
    bsum_o[...] = bsum


def _prep(p, mu, w0, a0, w2, a2, g2, k_k, k_a, r_k, *, n_rows, ctx_rows, image_rows):
    tm = PREP_TM
    hb = tm // HALO
    n_halo = n_rows // HALO

    def main(col0, width):
        return pl.BlockSpec((tm, width), lambda i: (i, col0 // width))

    def top(col0, width):
        return pl.BlockSpec((HALO, width), lambda i: (jnp.maximum(i * hb - 1, 0), col0 // width))

    def bot(col0, width):
        return pl.BlockSpec((HALO, width), lambda i: (jnp.minimum(i * hb + hb, n_halo - 1), col0 // width))

    def full(a):
        nd = a.ndim
        return pl.BlockSpec(a.shape, lambda i: (0,) * nd)

    in_specs = []
    for col0 in (COL_R, COL_K, COL_V):
        in_specs += [main(col0, D_A), top(col0, D_A), bot(col0, D_A)]
    in_specs += [main(COL_LORA, LORA_W), top(COL_LORA, LORA_W), bot(COL_LORA, LORA_W)]
    params = (mu, w0, a0, w2, a2, g2, k_k, k_a, r_k)
    in_specs += [full(a) for a in params]
    t_spec = pl.BlockSpec((D_A, tm), lambda i: (0, i))
    n_spec = pl.BlockSpec((tm, D_A), lambda i: (i, 0))
    t_shape = jax.ShapeDtypeStruct((D_A, n_rows), F32)
    n_shape = jax.ShapeDtypeStruct((n_rows, D_A), F32)
    return pl.pallas_call(
        functools.partial(_prep_kernel, ctx_tiles=ctx_rows // tm, tiles_per_image=image_rows // tm),
        grid=(n_rows // tm,),
        in_specs=in_specs,
        out_specs=[t_spec] * 9 + [n_spec, n_spec, pl.BlockSpec((tm, LANES), lambda i: (i, 0))],
        out_shape=[t_shape] * 9 + [n_shape, n_shape, jax.ShapeDtypeStruct((n_rows, LANES), F32)],
        compiler_params=_cparams(("arbitrary",)),
        name="rwkv_prep",
    )(p, p, p, p, p, p, p, p, p, p, p, p, *params)


RELAYOUT_T = 128
K_PER_BLOCK = LANES // N_HEADS


def _seq_col_block(stream, g, b, tb):
    row0, _, t_len = stream
    return (row0 + (g * SEQ_PER_GROUP + b) * t_len) // RELAYOUT_T + tb


def _to_chain_kernel(*refs, n_arrays):
    ins, outs = refs[:SEQ_PER_GROUP * n_arrays], refs[SEQ_PER_GROUP * n_arrays:]
    for a in range(n_arrays):
        blks = [ins[SEQ_PER_GROUP * a + b][...] for b in range(SEQ_PER_GROUP)]
        for kk in range(K_PER_BLOCK):
            slabs = [blk[kk * N_HEADS:(kk + 1) * N_HEADS] for blk in blks]
            outs[a][0, kk] = jnp.concatenate(slabs + slabs, axis=0).T


def _to_chain(arrays, stream):
    _, n_seq, t_len = stream
    groups = n_seq // SEQ_PER_GROUP
    n = len(arrays)
    in_specs, operands = [], []
    for a in arrays:
        for b in range(SEQ_PER_GROUP):
            in_specs.append(pl.BlockSpec(
                (LANES, RELAYOUT_T),
                functools.partial(lambda b, g, tb, kb: (kb, _seq_col_block(stream, g, b, tb)), b)))
            operands.append(a)
    out_spec = pl.BlockSpec((1, K_PER_BLOCK, RELAYOUT_T, LANES), lambda g, tb, kb: (g, kb, tb, 0))
    out_shape = jax.ShapeDtypeStruct((groups, HEAD, t_len, LANES), F32)
    return pl.pallas_call(
        functools.partial(_to_chain_kernel, n_arrays=n),
        grid=(groups, t_len // RELAYOUT_T, HEAD // K_PER_BLOCK),
        in_specs=in_specs,
        out_specs=[out_spec] * n,
        out_shape=[out_shape] * n,
        compiler_params=_cparams(("arbitrary", "arbitrary", "arbitrary")),
        name="to_chain",
    )(*operands)


def _value_to_chain_kernel(v0_ref, v1_ref, v2_ref, v3_ref, o_ref):
    blks = [r[...] for r in (v0_ref, v1_ref, v2_ref, v3_ref)]
    for vl in range(HALF):
        slabs = [blk[(vh * HALF + vl) * N_HEADS:(vh * HALF + vl + 1) * N_HEADS]
                 for vh in range(2) for blk in blks]
        o_ref[0, pl.ds(vl, RELAYOUT_T, stride=HALF), :] = jnp.concatenate(slabs, axis=0).T


def _value_to_chain(v_t, stream):
    _, n_seq, t_len = stream
    groups = n_seq // SEQ_PER_GROUP
    in_specs = [pl.BlockSpec((D_A, RELAYOUT_T),
                             functools.partial(lambda b, g, tb: (0, _seq_col_block(stream, g, b, tb)), b))
                for b in range(SEQ_PER_GROUP)]
    return pl.pallas_call(
        _value_to_chain_kernel,
        grid=(groups, t_len // RELAYOUT_T),
        in_specs=in_specs,
        out_specs=pl.BlockSpec((1, RELAYOUT_T * HALF, LANES), lambda g, tb: (g, tb, 0)),
        out_shape=jax.ShapeDtypeStruct((groups, t_len * HALF, LANES), F32),
        compiler_params=_cparams(("arbitrary", "arbitrary")),
        name="value_to_chain",
    )(v_t, v_t, v_t, v_t)


def _from_chain_kernel(z0_ref, z1_ref, o_ref, t_ref):
    b = pl.program_id(2)

    @pl.when(b == 0)
    def _():
        for vl in range(HALF):
            rows = pl.ds(vl, RELAYOUT_T, stride=HALF)
            t_ref[vl] = (z0_ref[0, rows, :] + z1_ref[0, rows, :]).T

    for vl in range(HALF):
        for vh in range(2):
            src = pl.ds(pl.multiple_of(vh * (LANES // 2) + b * N_HEADS, N_HEADS), N_HEADS)
            o_ref[(vh * HALF + vl) * N_HEADS:(vh * HALF + vl + 1) * N_HEADS, :] = t_ref[vl, src, :]


def _from_chain(z0, z1, stream, n_rows):
    _, n_seq, t_len = stream
    groups = n_seq // SEQ_PER_GROUP
    z_spec = pl.BlockSpec((1, RELAYOUT_T * HALF, LANES), lambda g, tb, b: (g, tb, 0))
    return pl.pallas_call(
        _from_chain_kernel,
        grid=(groups, t_len // RELAYOUT_T, SEQ_PER_GROUP),
        in_specs=[z_spec, z_spec],
        out_specs=pl.BlockSpec((D_A, RELAYOUT_T), lambda g, tb, b: (0, (g * SEQ_PER_GROUP + b) * (t_len // RELAYOUT_T) + tb)),
        out_shape=jax.ShapeDtypeStruct((D_A, n_seq * t_len), F32),
        scratch_shapes=[pltpu.VMEM((HALF, LANES, RELAYOUT_T), F32)],
        compiler_params=_cparams(("arbitrary", "arbitrary", "arbitrary")),
        name="from_chain",
    )(z0, z1)


SCAN_TT = 64
V_TILES = HALF // SUBLANES
ACC_PARTS = 2


def _scan_kernel(a_ref, w_ref, b_ref, kd_ref, r_ref, vv_ref, s0_ref, gnw_ref, gnb_ref,
                 z_ref, sfin_ref, s_ref, *, reverse):
    @pl.when(pl.program_id(1) == 0)
    def _():
        s_ref[...] = s0_ref[0]

    gnw = gnw_ref[...]
    gnb = gnb_ref[...]

    def block_row(i):
        return SCAN_TT - 1 - i if reverse else i

    def row(ref, k, ii):
        return ref[0, k, pl.ds(ii, 1), :]

    def value_rows(ii):
        start = ii * HALF
        return pl.ds(start if isinstance(start, int) else pl.multiple_of(start, HALF), HALF)

    def accumulate(parts, k, term):
        j = k % ACC_PARTS
        parts[j] = term if parts[j] is None else parts[j] + term

    def total(parts):
        out = parts[0]
        for p in parts[1:]:
            out = out + p
        return out

    def group_norm_store(y, ii):
        def head_mean(x):
            part = jnp.sum(jnp.sum(x, axis=0), axis=0, keepdims=True)
            return (part + pltpu.roll(part, LANES // 2, axis=1)) * (1.0 / HEAD)

        yc = y - head_mean(y)
        z = yc * lax.rsqrt(head_mean(yc * yc) + GN_EPS) * gnw + gnb
        z_ref[0, value_rows(ii), :] = z.reshape(HALF, LANES)

    first = block_row(0)
    parts = [None] * ACC_PARTS
    for k in range(HEAD):
        accumulate(parts, k, s_ref[k] * row(a_ref, k, first))

    def step(i, carry):
        sa, y_prev = carry
        ii = block_row(i)
        ii_next = block_row(jnp.minimum(i + 1, SCAN_TT - 1))
        group_norm_store(y_prev, block_row(jnp.maximum(i - 1, 0)))
        vv = vv_ref[0, value_rows(ii), :].reshape(V_TILES, SUBLANES, LANES)
        y_parts = [None] * ACC_PARTS
        sa_parts = [None] * ACC_PARTS
        for k in range(HEAD):
            s = s_ref[k] * row(w_ref, k, ii) + sa * row(b_ref, k, ii) + vv * row(kd_ref, k, ii)
            s_ref[k] = s
            accumulate(y_parts, k, s * row(r_ref, k, ii))
            accumulate(sa_parts, k, s * row(a_ref, k, ii_next))
        return total(sa_parts), total(y_parts)

    zero = jnp.zeros((V_TILES, SUBLANES, LANES), F32)
    _, y_last = lax.fori_loop(0, SCAN_TT, step, (total(parts), zero), unroll=2)
    group_norm_store(y_last, block_row(SCAN_TT - 1))

    @pl.when(pl.program_id(1) == pl.num_programs(1) - 1)
    def _():
        sfin_ref[0] = s_ref[...]


def _scan(a, w, b, kd, r, vv, s0, gnw, gnb, *, reverse):
    g, t = a.shape[0], a.shape[2]
    tt = SCAN_TT
    nt = t // tt
    tblk = (lambda ti: nt - 1 - ti) if reverse else (lambda ti: ti)
    kspec = pl.BlockSpec((1, HEAD, tt, LANES), lambda gi, ti: (gi, 0, tblk(ti), 0))
    vspec = pl.BlockSpec((1, tt * HALF, LANES), lambda gi, ti: (gi, tblk(ti), 0))
    sspec = pl.BlockSpec((1, HEAD, V_TILES, SUBLANES, LANES), lambda gi, ti: (gi, 0, 0, 0, 0))
    pspec = pl.BlockSpec((V_TILES, SUBLANES, LANES), lambda gi, ti: (0, 0, 0))
    return pl.pallas_call(
        functools.partial(_scan_kernel, reverse=reverse),
        grid=(g, nt),
        in_specs=[kspec] * 5 + [vspec, sspec, pspec, pspec],
        out_specs=[vspec, sspec],
        out_shape=[jax.ShapeDtypeStruct((g, t * HALF, LANES), F32),
                   jax.ShapeDtypeStruct((g, HEAD, V_TILES, SUBLANES, LANES), F32)],
        scratch_shapes=[pltpu.VMEM((HEAD, V_TILES, SUBLANES, LANES), F32)],
        compiler_params=_cparams(("arbitrary", "arbitrary")),
        name="delta_scan_bwd" if reverse else "delta_scan_fwd",
    )(a, w, b, kd, r, vv, s0, gnw, gnb)


def _gmlp_kernel(u_ref, v_ref, lng_ref, lnb_ref, ws_ref, bs_ref, o_ref):
    g = pl.program_id(1)
    v = _gelu_tanh(v_ref[...])
    mean = jnp.mean(v, axis=-1, keepdims=True)
    vc = v - mean
    var = jnp.mean(vc * vc, axis=-1, keepdims=True)
    vn = (vc * lax.rsqrt(var + LN_EPS) * lng_ref[pl.ds(g, 1), :] + lnb_ref[pl.ds(g, 1), :]).astype(BF16)
    u = _gelu_tanh(u_ref[...])
    ws = ws_ref[0]
    bs = bs_ref[0]
    for c in range(u.shape[0] // CHUNK):
        rows = slice(c * CHUNK, (c + 1) * CHUNK)
        s = jnp.dot(ws, vn[rows], preferred_element_type=F32) + bs
        o_ref[rows, :] = (u[rows] * s).astype(o_ref.dtype)


def _gmlp(p, ln_g, ln_b, w_sp, b_sp):
    tm = 512
    ub, gb = COL_U // LANES, COL_G // LANES
    return pl.pallas_call(
        _gmlp_kernel,
        grid=(N_ROWS // tm, N_GROUPS_B),
        in_specs=[
            pl.BlockSpec((tm, LANES), lambda i, g: (i, ub + g)),
            pl.BlockSpec((tm, LANES), lambda i, g: (i, gb + g)),
            pl.BlockSpec((N_GROUPS_B, LANES), lambda i, g: (0, 0)),
            pl.BlockSpec((N_GROUPS_B, LANES), lambda i, g: (0, 0)),
            pl.BlockSpec((1, CHUNK, CHUNK), lambda i, g: (g, 0, 0)),
            pl.BlockSpec((1, CHUNK, LANES), lambda i, g: (g, 0, 0)),
        ],
        out_specs=pl.BlockSpec((tm, LANES), lambda i, g: (i, g)),
        out_shape=jax.ShapeDtypeStruct((N_ROWS, D_B), BF16),
        compiler_params=_cparams(("arbitrary", "arbitrary")),
        name="chunk_gmlp",
    )(p, p, ln_g, ln_b, w_sp, b_sp)


def _outproj_kernel(x_ref, zc_ref, zl_ref, v_ref, bsum_ref, gate_ref, ob_ref, mod_ref, w_ref, o_ref, a_ref,
                    *, ctx_tiles):
    i = pl.program_id(0)

    @pl.when(pl.program_id(1) == 0)
    def _():
        def fill(z_ref):
            oa = (z_ref[...].T + _tile_heads(bsum_ref[...]) * v_ref[...]) * gate_ref[...]
            a_ref[:, :D_A] = oa.astype(BF16)

        pl.when(i < ctx_tiles)(lambda: fill(zc_ref))
        pl.when(i >= ctx_tiles)(lambda: fill(zl_ref))
        a_ref[:, D_A:] = ob_ref[...]

    tn = o_ref.shape[1]
    j = pl.program_id(1)
    g1 = mod_ref[0, 2:3, pl.ds(pl.multiple_of(j * tn, tn), tn)]
    o_ref[...] = x_ref[...] + g1 * jnp.dot(a_ref[...], w_ref[...], preferred_element_type=F32)


def _outproj(x, z_ctx, z_lat, v, bsum, gate, ob, mod_l, w):
    tm, tn = 512, 1024
    per_group = GROUP_ROWS // tm
    ctx_tiles = CTX_ROWS // tm
    n_lat_tiles = (N_ROWS - CTX_ROWS) // tm
    half = pl.BlockSpec((tm, D_A), lambda i, j: (i, 0))
    return pl.pallas_call(
        functools.partial(_outproj_kernel, ctx_tiles=ctx_tiles),
        grid=(N_ROWS // tm, D_MODEL // tn),
        in_specs=[
            pl.BlockSpec((tm, tn), lambda i, j: (i, j)),
            pl.BlockSpec((D_A, tm), lambda i, j: (0, jnp.minimum(i, ctx_tiles - 1))),
            pl.BlockSpec((D_A, tm), lambda i, j: (0, jnp.clip(i - ctx_tiles, 0, n_lat_tiles - 1))),
            half,
            pl.BlockSpec((tm, LANES), lambda i, j: (i, 0)),
            half, half,
            pl.BlockSpec((1, 6, D_MODEL), lambda i, j: (i // per_group, 0, 0)),
            pl.BlockSpec((D_MODEL, tn), lambda i, j: (0, j)),
        ],
        out_specs=pl.BlockSpec((tm, tn), lambda i, j: (i, j)),
        out_shape=jax.ShapeDtypeStruct((N_ROWS, D_MODEL), F32),
        scratch_shapes=[pltpu.VMEM((tm, D_MODEL), BF16)],
        compiler_params=_cparams(("arbitrary", "arbitrary")),
        name="outproj",
    )(x, z_ctx, z_lat, v, bsum, gate, ob, mod_l, w)


def _ffn_kernel(x_ref, mod_ref, g_ref, wg_ref, wu_ref, wd_ref, o_ref, h_ref, acc_ref):
    f = pl.program_id(1)

    @pl.when(f == 0)
    def _():
        m = mod_ref[0]
        h_ref[...] = _modnorm(x_ref[...], g_ref[...], m[4:5], m[3:4]).astype(BF16)
        acc_ref[...] = jnp.zeros_like(acc_ref)

    h = h_ref[...]
    gate = jnp.dot(h, wg_ref[...], preferred_element_type=F32)
    up = jnp.dot(h, wu_ref[...], preferred_element_type=F32)
    act = (_silu(gate) * up).astype(BF16)
    acc_ref[...] += jnp.dot(act, wd_ref[...], preferred_element_type=F32)

    @pl.when(f == pl.num_programs(1) - 1)
    def _():
        o_ref[...] = x_ref[...] + mod_ref[0, 5:6] * acc_ref[...]


def _ffn(x, mod_l, g, wg, wu, wd):
    tm, tf = 512, 512
    per_group = GROUP_ROWS // tm
    return pl.pallas_call(
        _ffn_kernel,
        grid=(N_ROWS // tm, D_FF // tf),
        in_specs=[
            pl.BlockSpec((tm, D_MODEL), lambda i, f: (i, 0)),
            pl.BlockSpec((1, 6, D_MODEL), lambda i, f: (i // per_group, 0, 0)),
            pl.BlockSpec((1, D_MODEL), lambda i, f: (0, 0)),
            pl.BlockSpec((D_MODEL, tf), lambda i, f: (0, f)),
            pl.BlockSpec((D_MODEL, tf), lambda i, f: (0, f)),
            pl.BlockSpec((tf, D_MODEL), lambda i, f: (f, 0)),
        ],
        out_specs=pl.BlockSpec((tm, D_MODEL), lambda i, f: (i, 0)),
        out_shape=jax.ShapeDtypeStruct((N_ROWS, D_MODEL), F32),
        scratch_shapes=[pltpu.VMEM((tm, D_MODEL), BF16), pltpu.VMEM((tm, D_MODEL), F32)],
        compiler_params=_cparams(("arbitrary", "arbitrary")),
        name="ffn",
    )(x, mod_l, g, wg, wu, wd)


def _final_norm_kernel(x_ref, g_ref, o_ref):
    x = x_ref[...]
    o_ref[...] = x * lax.rsqrt(jnp.mean(x * x, axis=-1, keepdims=True) + RMS_EPS) * g_ref[...]


def _final_norm(x, g):
    tm = 512
    return pl.pallas_call(
        _final_norm_kernel,
        grid=(N_ROWS // tm,),
        in_specs=[pl.BlockSpec((tm, D_MODEL), lambda i: (i, 0)),
                  pl.BlockSpec((1, D_MODEL), lambda i: (0, 0))],
        out_specs=pl.BlockSpec((tm, D_MODEL), lambda i: (i, 0)),
        out_shape=jax.ShapeDtypeStruct((N_ROWS, D_MODEL), F32),
        compiler_params=_cparams(("arbitrary",)),
        name="final_norm",
    )(x, g)


def _head_param_tile(p):
    t = p.T.reshape(2, HALF, 1, N_HEADS)
    t = jnp.broadcast_to(t, (2, HALF, SEQ_PER_GROUP, N_HEADS))
    return t.transpose(1, 0, 2, 3).reshape(V_TILES, SUBLANES, LANES)


def _state_to_chain(s):
    g = s.shape[0]
    s = s.reshape(g, SEQ_PER_GROUP, N_HEADS, 2, HALF, HEAD)
    return s.transpose(0, 5, 4, 3, 1, 2).reshape(g, HEAD, V_TILES, SUBLANES, LANES)


def _state_from_chain(s):
    g = s.shape[0]
    s = s.reshape(g, HEAD, HALF, 2, SEQ_PER_GROUP, N_HEADS)
    return s.transpose(0, 4, 5, 3, 2, 1).reshape(g * SEQ_PER_GROUP, N_HEADS, HEAD, HEAD)


def _rwkv_stream(t_arrays, stream, s0_fwd, s0_bwd, gnw_t, gnb_t, n_rows):
    nkk_t, dec0_t, dec1_t, b0_t, b1_t, kd0_t, kd1_t, r_t, v_t = t_arrays
    nkk_c, dec0_c, dec1_c, b0_c, b1_c, kd0_c, kd1_c, r_c = _to_chain(
        (nkk_t, dec0_t, dec1_t, b0_t, b1_t, kd0_t, kd1_t, r_t), stream)
    vv_c = _value_to_chain(v_t, stream)
    z0, sf0 = _scan(nkk_c, dec0_c, b0_c, kd0_c, r_c, vv_c, s0_fwd, gnw_t, gnb_t, reverse=False)
    z1, sf1 = _scan(nkk_c, dec1_c, b1_c, kd1_c, r_c, vv_c, s0_bwd, gnw_t, gnb_t, reverse=True)
    return _from_chain(z0, z1, stream, n_rows), sf0, sf1


def _index_major(n=N_HEADS, m=HEAD):
    return jnp.arange(n * m).reshape(n, m).T.reshape(-1)


def kernel(x_prompt, x_sample, state_rwkv, c, c_ctx, w_mod, b_mod, norm1_g, w_in, mu_shift, w0, w2, a0, a2,
           g2, k_k, k_a, r_k, gn_w, gn_b, gmlp_ln_g, gmlp_ln_b, w_spatial, b_spatial, w_out, norm2_g,
           w_ffn_gate, w_ffn_up, w_ffn_down, final_norm_g):
    x = jnp.concatenate([x_prompt.reshape(CTX_ROWS, D_MODEL), x_sample.reshape(N_LAT * LAT_LEN, D_MODEL)], axis=0)
    c_all = jnp.concatenate([c_ctx[None], c, jnp.zeros((SUBLANES - 1 - N_LAT, D_MODEL), F32)], axis=0)
    mod = _modulation(c_all, w_mod, b_mod).reshape(DEPTH, SUBLANES, 6, D_MODEL)
    perm = _index_major()

    def pad_cols(cols):
        c0 = 3 * D_A
        r, k, v = (cols[..., j * D_A:(j + 1) * D_A][..., perm] for j in range(3))
        wd = cols[..., c0:c0 + DECAY_LORA]
        ad = cols[..., c0 + DECAY_LORA:c0 + DECAY_LORA + A_LORA]
        gd = cols[..., c0 + DECAY_LORA + A_LORA:c0 + DECAY_LORA + A_LORA + GATE_LORA]
        ug = cols[..., c0 + DECAY_LORA + A_LORA + GATE_LORA:]
        z = lambda n: jnp.zeros(cols.shape[:-1] + (n,), cols.dtype)
        return jnp.concatenate([r, k, v, wd, z(LANES - DECAY_LORA), ad, z(LANES - A_LORA),
                                gd, z(2 * LANES - GATE_LORA), ug], axis=-1)

    def pad_rows(w, rows):
        return jnp.pad(w, [(0, 0)] * (w.ndim - 2) + [(0, rows - w.shape[-2]), (0, 0)])

    ctx_groups = N_CTX_SEQ // SEQ_PER_GROUP
    s0_ctx = jnp.zeros((ctx_groups, HEAD, V_TILES, SUBLANES, LANES), F32)
    ctx_states = []
    for l in range(DEPTH):
        mod_l = mod[l]
        p = _inproj(x, mod_l, norm1_g[l][None], pad_cols(w_in[l]).astype(BF16))

        outs = _prep(p, pad_cols(mu_shift[l])[:, :COL_U], w0[l][:, perm], a0[l][:, perm],
                     pad_rows(w2[l][..., perm], LANES).astype(BF16),
                     pad_rows(a2[l][..., perm], LANES).astype(BF16),
                     pad_rows(g2[l][:, perm], 2 * LANES).astype(BF16),
                     k_k[l][perm][None], k_a[l][perm][None], r_k[l].reshape(-1)[perm][None],
                     n_rows=N_ROWS, ctx_rows=CTX_ROWS, image_rows=LAT_LEN)
        t_arrays, (gate, v_nat, bsum) = outs[:9], outs[9:]
        gnw_t, gnb_t = _head_param_tile(gn_w[l]), _head_param_tile(gn_b[l])

        z_ctx, sf0, sf1 = _rwkv_stream(t_arrays, CTX_STREAM, s0_ctx, s0_ctx, gnw_t, gnb_t, N_ROWS)
        s_lat = state_rwkv[:, l]
        z_lat, _, _ = _rwkv_stream(t_arrays, LAT_STREAM, _state_to_chain(s_lat[None, :, 0]),
                                   _state_to_chain(s_lat[None, :, 1]), gnw_t, gnb_t, N_ROWS)
        ctx_states.append(jnp.stack([_state_from_chain(sf0), _state_from_chain(sf1)], axis=1))

        ob = _gmlp(p, gmlp_ln_g[l], gmlp_ln_b[l], w_spatial[l].astype(BF16),
                   jnp.broadcast_to(b_spatial[l][:, :, None], (N_GROUPS_B, CHUNK, LANES)))
        w_out_l = jnp.concatenate([w_out[l][:D_A][perm], w_out[l][D_A:]], axis=0).astype(BF16)
        x = _outproj(x, z_ctx, z_lat, v_nat, bsum, gate, ob, mod_l, w_out_l)
        x = _ffn(x, mod_l, norm2_g[l][None], w_ffn_gate[l].astype(BF16), w_ffn_up[l].astype(BF16),
                 w_ffn_down[l].astype(BF16))

    y = _final_norm(x, final_norm_g[None])
    y_prompt = y[:CTX_ROWS].reshape(N_CTX_SEQ, CTX_LEN, D_MODEL)
    y_sample = y[CTX_ROWS:].reshape(N_LAT, LAT_LEN, D_MODEL)
    new_state = jnp.stack(ctx_states, axis=1)
    return (y_prompt, y_sample, new_state)
```

```python
import functools
import math

import jax
import jax.numpy as jnp
from jax import lax
from jax.experimental import pallas as pl
from jax.experimental.pallas import tpu as pltpu

F32 = jnp.float32
BF16 = jnp.bfloat16

D_MODEL = 2048
DEPTH = 4
N_CTX_SEQ = 16
CTX_LEN = 256
N_LAT = 4
LAT_LEN = 4096
GRID_W = 64
D_A = 1024
HEAD = 64
N_HEADS = 16
D_B = 1024
N_GROUPS_B = 8
CHUNK = 128
DECAY_LORA = 64
A_LORA = 64
GATE_LORA = 160
D_FF = 5632
RMS_EPS = 1e-6
GN_EPS = HEAD * 1e-5
LN_EPS = 1e-5

GROUP_ROWS = 4096
CTX_ROWS = N_CTX_SEQ * CTX_LEN
N_ROWS = CTX_ROWS + N_LAT * LAT_LEN
LANES = 128
SUBLANES = 8
SEQ_PER_GROUP = 4
HALF = HEAD // 2

COL_R, COL_K, COL_V = 0, D_A, 2 * D_A
COL_LORA = 3 * D_A
LORA_W = 512
COL_U = COL_LORA + LORA_W
COL_G = COL_U + D_B
P_PAD = COL_G + D_B

VMEM_LIMIT = 48 * 1024 * 1024

CTX_STREAM = (0, N_CTX_SEQ, CTX_LEN)
LAT_STREAM = (CTX_ROWS, N_LAT, LAT_LEN)


def _cparams(sem):
    return pltpu.CompilerParams(dimension_semantics=sem, vmem_limit_bytes=VMEM_LIMIT)


def _silu(x):
    return x * jax.nn.sigmoid(x)


def _gelu_tanh(x):
    return 0.5 * x * (1.0 + jnp.tanh(math.sqrt(2.0 / math.pi) * (x + 0.044715 * (x * x * x))))


def _modnorm(x, g, scale, shift):
    y = x * lax.rsqrt(jnp.mean(x * x, axis=-1, keepdims=True) + RMS_EPS)
    return (y * g) * (1.0 + scale) + shift


def _mod_kernel(c_ref, w_ref, b_ref, o_ref):
    a = _silu(c_ref[...]).astype(BF16)
    o_ref[0] = jnp.dot(a, w_ref[0].astype(BF16), preferred_element_type=F32) + b_ref[0]


def _modulation(c_all, w_mod, b_mod):
    tn = 1024
    n6 = 6 * D_MODEL
    return pl.pallas_call(
        _mod_kernel,
        grid=(DEPTH, n6 // tn),
        in_specs=[
            pl.BlockSpec((SUBLANES, D_MODEL), lambda l, j: (0, 0)),
            pl.BlockSpec((1, D_MODEL, tn), lambda l, j: (l, 0, j)),
            pl.BlockSpec((1, 1, tn), lambda l, j: (l, 0, j)),
        ],
        out_specs=pl.BlockSpec((1, SUBLANES, tn), lambda l, j: (l, 0, j)),
        out_shape=jax.ShapeDtypeStruct((DEPTH, SUBLANES, n6), F32),
        compiler_params=_cparams(("arbitrary", "arbitrary")),
        name="modulation",
    )(c_all, w_mod, b_mod.reshape(DEPTH, 1, n6))


def _inproj_kernel(x_ref, mod_ref, g_ref, w_ref, o_ref, h_ref):
    @pl.when(pl.program_id(1) == 0)
    def _():
        m = mod_ref[0]
        h_ref[...] = _modnorm(x_ref[...], g_ref[...], m[1:2], m[0:1]).astype(BF16)

    o_ref[...] = jnp.dot(h_ref[...], w_ref[...], preferred_element_type=F32)


def _inproj(x, mod_l, g, w):
    tm, tn = 1024, 512
    per_group = GROUP_ROWS // tm
    return pl.pallas_call(
        _inproj_kernel,
        grid=(N_ROWS // tm, P_PAD // tn),
        in_specs=[
            pl.BlockSpec((tm, D_MODEL), lambda i, j: (i, 0)),
            pl.BlockSpec((1, 6, D_MODEL), lambda i, j: (i // per_group, 0, 0)),
            pl.BlockSpec((1, D_MODEL), lambda i, j: (0, 0)),
            pl.BlockSpec((D_MODEL, tn), lambda i, j: (0, j)),
        ],
        out_specs=pl.BlockSpec((tm, tn), lambda i, j: (i, j)),
        out_shape=jax.ShapeDtypeStruct((N_ROWS, P_PAD), F32),
        scratch_shapes=[pltpu.VMEM((tm, D_MODEL), BF16)],
        compiler_params=_cparams(("arbitrary", "arbitrary")),
        name="inproj",
    )(x, mod_l, g, w)


PREP_TM = 256
HALO = GRID_W


def _shift(x, top, bot, mu, row, is_ctx, has_top, has_bot):
    tm = x.shape[0]
    period = jnp.where(is_ctx, CTX_LEN, GRID_W)
    pos = row & (period - 1)
    local = lax.broadcasted_iota(jnp.int32, (tm, 1), 0)
    prev = jnp.where(local == 0, top[HALO - 1:HALO], pltpu.roll(x, 1, axis=0))
    nxt = jnp.where(local == tm - 1, bot[0:1], pltpu.roll(x, tm - 1, axis=0))
    prev = jnp.where(pos == 0, 0.0, prev)
    nxt = jnp.where(pos == period - 1, 0.0, nxt)
    out = x + mu[0:1] * (prev - x) + mu[1:2] * (nxt - x)
    up = jnp.concatenate([jnp.where(has_top, top, 0.0), x[: tm - HALO]], axis=0)
    down = jnp.concatenate([x[HALO:], jnp.where(has_bot, bot, 0.0)], axis=0)
    grid_terms = mu[2:3] * (up - x) + mu[3:4] * (down - x)
    return out + jnp.where(is_ctx, 0.0, grid_terms)


def _head_sum(x):
    s = x[:, 0:LANES]
    for j in range(1, D_A // LANES):
        s = s + x[:, j * LANES:(j + 1) * LANES]
    for sh in (N_HEADS, 2 * N_HEADS, 4 * N_HEADS):
        s = s + pltpu.roll(s, sh, axis=1)
    return s


def _tile_heads(s):
    return jnp.concatenate([s] * (D_A // LANES), axis=1)


def _softplus(x):
    return jnp.maximum(x, 0.0) + jnp.log1p(jnp.exp(-jnp.abs(x)))


def _prep_kernel(r_ref, rt_ref, rb_ref, k_ref, kt_ref, kb_ref, v_ref, vt_ref, vb_ref,
                 l_ref, lt_ref, lb_ref, mu_ref, w0_ref, a0_ref, w2_ref, a2_ref, g2_ref,
                 kk_ref, ka_ref, rk_ref,
                 nkk_o, dec0_o, dec1_o, b0_o, b1_o, kd0_o, kd1_o, r_o, v_o, gate_o, vnat_o, bsum_o,
                 *, ctx_tiles, tiles_per_image):
    i = pl.program_id(0)
    is_ctx = i < ctx_tiles
    img_tile = (i - ctx_tiles) % tiles_per_image
    has_top = img_tile != 0
    has_bot = img_tile != tiles_per_image - 1
    row = i * PREP_TM + lax.broadcasted_iota(jnp.int32, (PREP_TM, 1), 0)
    sh = functools.partial(_shift, row=row, is_ctx=is_ctx, has_top=has_top, has_bot=has_bot)

    r = sh(r_ref[...], rt_ref[...], rb_ref[...], mu_ref[:, COL_R:COL_R + D_A])
    k = sh(k_ref[...], kt_ref[...], kb_ref[...], mu_ref[:, COL_K:COL_K + D_A])
    v = sh(v_ref[...], vt_ref[...], vb_ref[...], mu_ref[:, COL_V:COL_V + D_A])
    lo = sh(l_ref[...], lt_ref[...], lb_ref[...], mu_ref[:, COL_LORA:COL_LORA + LORA_W])
    wt = jnp.tanh(lo[:, 0:LANES]).astype(BF16)
    ad = lo[:, LANES:2 * LANES].astype(BF16)
    gs = jax.nn.sigmoid(lo[:, 2 * LANES:]).astype(BF16)

    kk = k * kk_ref[...]
    inv = lax.rsqrt(jnp.maximum(_head_sum(kk * kk), 1e-24))
    kk = kk * _tile_heads(inv)
    r_o[...] = r.T
    v_o[...] = v.T
    vnat_o[...] = v
    nkk_o[...] = (-kk).T
    gate_o[...] = jnp.dot(gs, g2_ref[...], preferred_element_type=F32)
    ka = ka_ref[...]
    rrk = r * rk_ref[...]
    bsum = None
    for d, (dec_o, b_o, kd_o) in enumerate(((dec0_o, b0_o, kd0_o), (dec1_o, b1_o, kd1_o))):
        wl = w0_ref[d:d + 1] + jnp.dot(wt, w2_ref[d], preferred_element_type=F32)
        w_log = -_softplus(-wl) - 0.5
        dec_o[...] = jnp.exp(-jnp.exp(w_log)).T
        a = jax.nn.sigmoid(a0_ref[d:d + 1] + jnp.dot(ad, a2_ref[d], preferred_element_type=F32))
        kd = k * (1.0 + (a - 1.0) * ka)
        kd_o[...] = kd.T
        b_o[...] = (kk * a).T
        bonus = _head_sum(rrk * kd)
        bsum = bonus if bsum is None else bsum + bonus
    bsum_o[...] = bsum


def _prep(p, mu, w0, a0, w2, a2, g2, k_k, k_a, r_k, *, n_rows, ctx_rows, image_rows):
    tm = PREP_TM
    hb = tm // HALO
    n_halo = n_rows // HALO

    def main(col0, width):
        return pl.BlockSpec((tm, width), lambda i: (i, col0 // width))

    def top(col0, width):
        return pl.BlockSpec((HALO, width), lambda i: (jnp.maximum(i * hb - 1, 0), col0 // width))

    def bot(col0, width):
        return pl.BlockSpec((HALO, width), lambda i: (jnp.minimum(i * hb + hb, n_halo - 1), col0 // width))

    def full(a):
        nd = a.ndim
        return pl.BlockSpec(a.shape, lambda i: (0,) * nd)

    in_specs = []
    for col0 in (COL_R, COL_K, COL_V):
        in_specs += [main(col0, D_A), top(col0, D_A), bot(col0, D_A)]
    in_specs += [main(COL_LORA, LORA_W), top(COL_LORA, LORA_W), bot(COL_LORA, LORA_W)]
    params = (mu, w0, a0, w2, a2, g2, k_k, k_a, r_k)
    in_specs += [full(a) for a in params]
    t_spec = pl.BlockSpec((D_A, tm), lambda i: (0, i))
    n_spec = pl.BlockSpec((tm, D_A), lambda i: (i, 0))
    t_shape = jax.ShapeDtypeStruct((D_A, n_rows), F32)
    n_shape = jax.ShapeDtypeStruct((n_rows, D_A), F32)
    return pl.pallas_call(
        functools.partial(_prep_kernel, ctx_tiles=ctx_rows // tm, tiles_per_image=image_rows // tm),
        grid=(n_rows // tm,),
        in_specs=in_specs,
        out_specs=[t_spec] * 9 + [n_spec, n_spec, pl.BlockSpec((tm, LANES), lambda i: (i, 0))],
        out_shape=[t_shape] * 9 + [n_shape, n_shape, jax.ShapeDtypeStruct((n_rows, LANES), F32)],
        compiler_params=_cparams(("arbitrary",)),
        name="rwkv_prep",
    )(p, p, p, p, p, p, p, p, p, p, p, p, *params)


RELAYOUT_T = 256
FROM_CHAIN_T = 128
K_PER_BLOCK = LANES // N_HEADS


def _seq_col_block(stream, g, b, tb):
    row0, _, t_len = stream
    return (row0 + (g * SEQ_PER_GROUP + b) * t_len) // RELAYOUT_T + tb


def _to_chain_kernel(*refs, n_arrays):
    ins, outs = refs[:SEQ_PER_GROUP * n_arrays], refs[SEQ_PER_GROUP * n_arrays:]
    for a in range(n_arrays):
        blks = [ins[SEQ_PER_GROUP * a + b][...] for b in range(SEQ_PER_GROUP)]
        for kk in range(K_PER_BLOCK):
            slabs = [blk[kk * N_HEADS:(kk + 1) * N_HEADS] for blk in blks]
            outs[a][0, kk] = jnp.concatenate(slabs + slabs, axis=0).T


def _to_chain(arrays, stream):
    _, n_seq, t_len = stream
    groups = n_seq // SEQ_PER_GROUP
    n = len(arrays)
    in_specs, operands = [], []
    for a in arrays:
        for b in range(SEQ_PER_GROUP):
            in_specs.append(pl.BlockSpec(
                (LANES, RELAYOUT_T),
                functools.partial(lambda b, g, tb, kb: (kb, _seq_col_block(stream, g, b, tb)), b)))
            operands.append(a)
    out_spec = pl.BlockSpec((1, K_PER_BLOCK, RELAYOUT_T, LANES), lambda g, tb, kb: (g, kb, tb, 0))
    out_shape = jax.ShapeDtypeStruct((groups, HEAD, t_len, LANES), F32)
    return pl.pallas_call(
        functools.partial(_to_chain_kernel, n_arrays=n),
        grid=(groups, t_len // RELAYOUT_T, HEAD // K_PER_BLOCK),
        in_specs=in_specs,
        out_specs=[out_spec] * n,
        out_shape=[out_shape] * n,
        compiler_params=_cparams(("arbitrary", "arbitrary", "arbitrary")),
        name="to_chain",
    )(*operands)


def _value_to_chain_kernel(v0_ref, v1_ref, v2_ref, v3_ref, o_ref):
    blks = [r[...] for r in (v0_ref, v1_ref, v2_ref, v3_ref)]
    for vl in range(HALF):
        slabs = [blk[(vh * HALF + vl) * N_HEADS:(vh * HALF + vl + 1) * N_HEADS]
                 for vh in range(2) for blk in blks]
        o_ref[0, pl.ds(vl, RELAYOUT_T, stride=HALF), :] = jnp.concatenate(slabs, axis=0).T


def _value_to_chain(v_t, stream):
    _, n_seq, t_len = stream
    groups = n_seq // SEQ_PER_GROUP
    in_specs = [pl.BlockSpec((D_A, RELAYOUT_T),
                             functools.partial(lambda b, g, tb: (0, _seq_col_block(stream, g, b, tb)), b))
                for b in range(SEQ_PER_GROUP)]
    return pl.pallas_call(
        _value_to_chain_kernel,
        grid=(groups, t_len // RELAYOUT_T),
        in_specs=in_specs,
        out_specs=pl.BlockSpec((1, RELAYOUT_T * HALF, LANES), lambda g, tb: (g, tb, 0)),
        out_shape=jax.ShapeDtypeStruct((groups, t_len * HALF, LANES), F32),
        compiler_params=_cparams(("arbitrary", "arbitrary")),
        name="value_to_chain",
    )(v_t, v_t, v_t, v_t)


def _from_chain_kernel(z0_ref, z1_ref, o_ref, t_ref):
    b = pl.program_id(2)

    @pl.when(b == 0)
    def _():
        for vl in range(HALF):
            rows = pl.ds(vl, FROM_CHAIN_T, stride=HALF)
            t_ref[vl] = (z0_ref[0, rows, :] + z1_ref[0, rows, :]).T

    for vl in range(HALF):
        for vh in range(2):
            src = pl.ds(pl.multiple_of(vh * (LANES // 2) + b * N_HEADS, N_HEADS), N_HEADS)
            o_ref[(vh * HALF + vl) * N_HEADS:(vh * HALF + vl + 1) * N_HEADS, :] = t_ref[vl, src, :]


def _from_chain(z0, z1, stream, n_rows):
    _, n_seq, t_len = stream
    groups = n_seq // SEQ_PER_GROUP
    nt = t_len // FROM_CHAIN_T
    z_spec = pl.BlockSpec((1, FROM_CHAIN_T * HALF, LANES), lambda g, tb, b: (g, tb, 0))
    return pl.pallas_call(
        _from_chain_kernel,
        grid=(groups, nt, SEQ_PER_GROUP),
        in_specs=[z_spec, z_spec],
        out_specs=pl.BlockSpec((D_A, FROM_CHAIN_T), lambda g, tb, b: (0, (g * SEQ_PER_GROUP + b) * nt + tb)),
        out_shape=jax.ShapeDtypeStruct((D_A, n_seq * t_len), F32),
        scratch_shapes=[pltpu.VMEM((HALF, LANES, FROM_CHAIN_T), F32)],
        compiler_params=_cparams(("arbitrary", "arbitrary", "arbitrary")),
        name="from_chain",
    )(z0, z1)


SCAN_TT = 64
V_TILES = HALF // SUBLANES
ACC_PARTS = 2


def _scan_kernel(a_ref, w_ref, b_ref, kd_ref, r_ref, vv_ref, s0_ref, gnw_ref, gnb_ref,
                 z_ref, sfin_ref, s_ref, *, reverse):
    @pl.when(pl.program_id(1) == 0)
    def _():
        s_ref[...] = s0_ref[0]

    gnw = gnw_ref[...]
    gnb = gnb_ref[...]

    def block_row(i):
        return SCAN_TT - 1 - i if reverse else i

    def row(ref, k, ii):
        return ref[0, k, pl.ds(ii, 1), :]

    def value_rows(ii):
        start = ii * HALF
        return pl.ds(start if isinstance(start, int) else pl.multiple_of(start, HALF), HALF)

    def accumulate(parts, k, term):
        j = k % ACC_PARTS
        parts[j] = term if parts[j] is None else parts[j] + term

    def total(parts):
        out = parts[0]
        for p in parts[1:]:
            out = out + p
        return out

    def group_norm_store(y, ii):
        def head_mean(x):
            part = jnp.sum(jnp.sum(x, axis=0), axis=0, keepdims=True)
            return (part + pltpu.roll(part, LANES // 2, axis=1)) * (1.0 / HEAD)

        yc = y - head_mean(y)
        z = yc * lax.rsqrt(head_mean(yc * yc) + GN_EPS) * gnw + gnb
        z_ref[0, value_rows(ii), :] = z.reshape(HALF, LANES)

    first = block_row(0)
    parts = [None] * ACC_PARTS
    for k in range(HEAD):
        accumulate(parts, k, s_ref[k] * row(a_ref, k, first))

    def step(i, carry):
        sa, y_prev = carry
        ii = block_row(i)
        ii_next = block_row(jnp.minimum(i + 1, SCAN_TT - 1))
        group_norm_store(y_prev, block_row(jnp.maximum(i - 1, 0)))
        vv = vv_ref[0, value_rows(ii), :].reshape(V_TILES, SUBLANES, LANES)
        y_parts = [None] * ACC_PARTS
        sa_parts = [None] * ACC_PARTS
        for k in range(HEAD):
            s = s_ref[k] * row(w_ref, k, ii) + sa * row(b_ref, k, ii) + vv * row(kd_ref, k, ii)
            s_ref[k] = s
            accumulate(y_parts, k, s * row(r_ref, k, ii))
            accumulate(sa_parts, k, s * row(a_ref, k, ii_next))
        return total(sa_parts), total(y_parts)

    zero = jnp.zeros((V_TILES, SUBLANES, LANES), F32)
    _, y_last = lax.fori_loop(0, SCAN_TT, step, (total(parts), zero), unroll=4)
    group_norm_store(y_last, block_row(SCAN_TT - 1))

    @pl.when(pl.program_id(1) == pl.num_programs(1) - 1)
    def _():
        sfin_ref[0] = s_ref[...]


def _scan(a, w, b, kd, r, vv, s0, gnw, gnb, *, reverse):
    g, t = a.shape[0], a.shape[2]
    tt = SCAN_TT
    nt = t // tt
    tblk = (lambda ti: nt - 1 - ti) if reverse else (lambda ti: ti)
    kspec = pl.BlockSpec((1, HEAD, tt, LANES), lambda gi, ti: (gi, 0, tblk(ti), 0))
    vspec = pl.BlockSpec((1, tt * HALF, LANES), lambda gi, ti: (gi, tblk(ti), 0))
    sspec = pl.BlockSpec((1, HEAD, V_TILES, SUBLANES, LANES), lambda gi, ti: (gi, 0, 0, 0, 0))
    pspec = pl.BlockSpec((V_TILES, SUBLANES, LANES), lambda gi, ti: (0, 0, 0))
    return pl.pallas_call(
        functools.partial(_scan_kernel, reverse=reverse),
        grid=(g, nt),
        in_specs=[kspec] * 5 + [vspec, sspec, pspec, pspec],
        out_specs=[vspec, sspec],
        out_shape=[jax.ShapeDtypeStruct((g, t * HALF, LANES), F32),
                   jax.ShapeDtypeStruct((g, HEAD, V_TILES, SUBLANES, LANES), F32)],
        scratch_shapes=[pltpu.VMEM((HEAD, V_TILES, SUBLANES, LANES), F32)],
        compiler_params=_cparams(("arbitrary", "arbitrary")),
        name="delta_scan_bwd" if reverse else "delta_scan_fwd",
    )(a, w, b, kd, r, vv, s0, gnw, gnb)


def _gmlp_kernel(u_ref, v_ref, lng_ref, lnb_ref, ws_ref, bs_ref, o_ref):
    g = pl.program_id(1)
    v = _gelu_tanh(v_ref[...])
    mean = jnp.mean(v, axis=-1, keepdims=True)
    vc = v - mean
    var = jnp.mean(vc * vc, axis=-1, keepdims=True)
    vn = (vc * lax.rsqrt(var + LN_EPS) * lng_ref[pl.ds(g, 1), :] + lnb_ref[pl.ds(g, 1), :]).astype(BF16)
    u = _gelu_tanh(u_ref[...])
    ws = ws_ref[0]
    bs = bs_ref[0]
    for c in range(u.shape[0] // CHUNK):
        rows = slice(c * CHUNK, (c + 1) * CHUNK)
        s = jnp.dot(ws, vn[rows], preferred_element_type=F32) + bs
        o_ref[rows, :] = (u[rows] * s).astype(o_ref.dtype)


def _gmlp(p, ln_g, ln_b, w_sp, b_sp):
    tm = 1024
    ub, gb = COL_U // LANES, COL_G // LANES
    return pl.pallas_call(
        _gmlp_kernel,
        grid=(N_ROWS // tm, N_GROUPS_B),
        in_specs=[
            pl.BlockSpec((tm, LANES), lambda i, g: (i, ub + g)),
            pl.BlockSpec((tm, LANES), lambda i, g: (i, gb + g)),
            pl.BlockSpec((N_GROUPS_B, LANES), lambda i, g: (0, 0)),
            pl.BlockSpec((N_GROUPS_B, LANES), lambda i, g: (0, 0)),
            pl.BlockSpec((1, CHUNK, CHUNK), lambda i, g: (g, 0, 0)),
            pl.BlockSpec((1, CHUNK, LANES), lambda i, g: (g, 0, 0)),
        ],
        out_specs=pl.BlockSpec((tm, LANES), lambda i, g: (i, g)),
        out_shape=jax.ShapeDtypeStruct((N_ROWS, D_B), BF16),
        compiler_params=_cparams(("arbitrary", "arbitrary")),
        name="chunk_gmlp",
    )(p, p, ln_g, ln_b, w_sp, b_sp)


def _outproj_kernel(x_ref, zc_ref, zl_ref, v_ref, bsum_ref, gate_ref, ob_ref, mod_ref, w_ref, o_ref, a_ref,
                    *, ctx_tiles):
    i = pl.program_id(0)

    @pl.when(pl.program_id(1) == 0)
    def _():
        def fill(z_ref):
            oa = (z_ref[...].T + _tile_heads(bsum_ref[...]) * v_ref[...]) * gate_ref[...]
            a_ref[:, :D_A] = oa.astype(BF16)

        pl.when(i < ctx_tiles)(lambda: fill(zc_ref))
        pl.when(i >= ctx_tiles)(lambda: fill(zl_ref))
        a_ref[:, D_A:] = ob_ref[...]

    tn = o_ref.shape[1]
    j = pl.program_id(1)
    g1 = mod_ref[0, 2:3, pl.ds(pl.multiple_of(j * tn, tn), tn)]
    o_ref[...] = x_ref[...] + g1 * jnp.dot(a_ref[...], w_ref[...], preferred_element_type=F32)


def _outproj(x, z_ctx, z_lat, v, bsum, gate, ob, mod_l, w):
    tm, tn = 512, 1024
    per_group = GROUP_ROWS // tm
    ctx_tiles = CTX_ROWS // tm
    n_lat_tiles = (N_ROWS - CTX_ROWS) // tm
    half = pl.BlockSpec((tm, D_A), lambda i, j: (i, 0))
    return pl.pallas_call(
        functools.partial(_outproj_kernel, ctx_tiles=ctx_tiles),
        grid=(N_ROWS // tm, D_MODEL // tn),
        in_specs=[
            pl.BlockSpec((tm, tn), lambda i, j: (i, j)),
            pl.BlockSpec((D_A, tm), lambda i, j: (0, jnp.minimum(i, ctx_tiles - 1))),
            pl.BlockSpec((D_A, tm), lambda i, j: (0, jnp.clip(i - ctx_tiles, 0, n_lat_tiles - 1))),
            half,
            pl.BlockSpec((tm, LANES), lambda i, j: (i, 0)),
            half, half,
            pl.BlockSpec((1, 6, D_MODEL), lambda i, j: (i // per_group, 0, 0)),
            pl.BlockSpec((D_MODEL, tn), lambda i, j: (0, j)),
        ],
        out_specs=pl.BlockSpec((tm, tn), lambda i, j: (i, j)),
        out_shape=jax.ShapeDtypeStruct((N_ROWS, D_MODEL), F32),
        scratch_shapes=[pltpu.VMEM((tm, D_MODEL), BF16)],
        compiler_params=_cparams(("arbitrary", "arbitrary")),
        name="outproj",
    )(x, z_ctx, z_lat, v, bsum, gate, ob, mod_l, w)


def _ffn_kernel(x_ref, mod_ref, g_ref, wg_ref, wu_ref, wd_ref, o_ref, h_ref, acc_ref):
    f = pl.program_id(1)

    @pl.when(f == 0)
    def _():
        m = mod_ref[0]
        h_ref[...] = _modnorm(x_ref[...], g_ref[...], m[4:5], m[3:4]).astype(BF16)
        acc_ref[...] = jnp.zeros_like(acc_ref)

    h = h_ref[...]
    gate = jnp.dot(h, wg_ref[...], preferred_element_type=F32)
    up = jnp.dot(h, wu_ref[...], preferred_element_type=F32)
    act = (_silu(gate) * up).astype(BF16)
    acc_ref[...] += jnp.dot(act, wd_ref[...], preferred_element_type=F32)

    @pl.when(f == pl.num_programs(1) - 1)
    def _():
        o_ref[...] = x_ref[...] + mod_ref[0, 5:6] * acc_ref[...]


def _ffn(x, mod_l, g, wg, wu, wd):
    tm, tf = 512, 512
    per_group = GROUP_ROWS // tm
    return pl.pallas_call(
        _ffn_kernel,
        grid=(N_ROWS // tm, D_FF // tf),
        in_specs=[
            pl.BlockSpec((tm, D_MODEL), lambda i, f: (i, 0)),
            pl.BlockSpec((1, 6, D_MODEL), lambda i, f: (i // per_group, 0, 0)),
            pl.BlockSpec((1, D_MODEL), lambda i, f: (0, 0)),
            pl.BlockSpec((D_MODEL, tf), lambda i, f: (0, f)),
            pl.BlockSpec((D_MODEL, tf), lambda i, f: (0, f)),
            pl.BlockSpec((tf, D_MODEL), lambda i, f: (f, 0)),
        ],
        out_specs=pl.BlockSpec((tm, D_MODEL), lambda i, f: (i, 0)),
        out_shape=jax.ShapeDtypeStruct((N_ROWS, D_MODEL), F32),
        scratch_shapes=[pltpu.VMEM((tm, D_MODEL), BF16), pltpu.VMEM((tm, D_MODEL), F32)],
        compiler_params=_cparams(("arbitrary", "arbitrary")),
        name="ffn",
    )(x, mod_l, g, wg, wu, wd)


def _final_norm_kernel(x_ref, g_ref, o_ref):
    x = x_ref[...]
    o_ref[...] = x * lax.rsqrt(jnp.mean(x * x, axis=-1, keepdims=True) + RMS_EPS) * g_ref[...]


def _final_norm(x, g, row0, rows):
    tm = 512
    return pl.pallas_call(
        _final_norm_kernel,
        grid=(rows // tm,),
        in_specs=[pl.BlockSpec((tm, D_MODEL), lambda i: (row0 // tm + i, 0)),
                  pl.BlockSpec((1, D_MODEL), lambda i: (0, 0))],
        out_specs=pl.BlockSpec((tm, D_MODEL), lambda i: (i, 0)),
        out_shape=jax.ShapeDtypeStruct((rows, D_MODEL), F32),
        compiler_params=_cparams(("arbitrary",)),
        name="final_norm",
    )(x, g)


def _head_param_tile(p):
    t = p.T.reshape(2, HALF, 1, N_HEADS)
    t = jnp.broadcast_to(t, (2, HALF, SEQ_PER_GROUP, N_HEADS))
    return t.transpose(1, 0, 2, 3).reshape(V_TILES, SUBLANES, LANES)


def _state_to_chain(s):
    g = s.shape[0]
    s = s.reshape(g, SEQ_PER_GROUP, N_HEADS, 2, HALF, HEAD)
    return s.transpose(0, 5, 4, 3, 1, 2).reshape(g, HEAD, V_TILES, SUBLANES, LANES)


def _state_from_chain(s):
    g = s.shape[0]
    s = s.reshape(g, HEAD, HALF, 2, SEQ_PER_GROUP, N_HEADS)
    return s.transpose(0, 4, 5, 3, 2, 1).reshape(g * SEQ_PER_GROUP, N_HEADS, HEAD, HEAD)


def _rwkv_stream(t_arrays, stream, s0_fwd, s0_bwd, gnw_t, gnb_t, n_rows):
    nkk_t, dec0_t, dec1_t, b0_t, b1_t, kd0_t, kd1_t, r_t, v_t = t_arrays
    nkk_c, dec0_c, dec1_c, b0_c, b1_c, kd0_c, kd1_c, r_c = _to_chain(
        (nkk_t, dec0_t, dec1_t, b0_t, b1_t, kd0_t, kd1_t, r_t), stream)
    vv_c = _value_to_chain(v_t, stream)
    z0, sf0 = _scan(nkk_c, dec0_c, b0_c, kd0_c, r_c, vv_c, s0_fwd, gnw_t, gnb_t, reverse=False)
    z1, sf1 = _scan(nkk_c, dec1_c, b1_c, kd1_c, r_c, vv_c, s0_bwd, gnw_t, gnb_t, reverse=True)
    return _from_chain(z0, z1, stream, n_rows), sf0, sf1


def _index_major(n=N_HEADS, m=HEAD):
    return jnp.arange(n * m).reshape(n, m).T.reshape(-1)


def kernel(x_prompt, x_sample, state_rwkv, c, c_ctx, w_mod, b_mod, norm1_g, w_in, mu_shift, w0, w2, a0, a2,
           g2, k_k, k_a, r_k, gn_w, gn_b, gmlp_ln_g, gmlp_ln_b, w_spatial, b_spatial, w_out, norm2_g,
           w_ffn_gate, w_ffn_up, w_ffn_down, final_norm_g):
    x = jnp.concatenate([x_prompt.reshape(CTX_ROWS, D_MODEL), x_sample.reshape(N_LAT * LAT_LEN, D_MODEL)], axis=0)
    c_all = jnp.concatenate([c_ctx[None], c, jnp.zeros((SUBLANES - 1 - N_LAT, D_MODEL), F32)], axis=0)
    mod = _modulation(c_all, w_mod, b_mod).reshape(DEPTH, SUBLANES, 6, D_MODEL)
    perm = _index_major()

    def pad_cols(cols):
        c0 = 3 * D_A
        r, k, v = (cols[..., j * D_A:(j + 1) * D_A][..., perm] for j in range(3))
        wd = cols[..., c0:c0 + DECAY_LORA]
        ad = cols[..., c0 + DECAY_LORA:c0 + DECAY_LORA + A_LORA]
        gd = cols[..., c0 + DECAY_LORA + A_LORA:c0 + DECAY_LORA + A_LORA + GATE_LORA]
        ug = cols[..., c0 + DECAY_LORA + A_LORA + GATE_LORA:]
        z = lambda n: jnp.zeros(cols.shape[:-1] + (n,), cols.dtype)
        return jnp.concatenate([r, k, v, wd, z(LANES - DECAY_LORA), ad, z(LANES - A_LORA),
                                gd, z(2 * LANES - GATE_LORA), ug], axis=-1)

    def pad_rows(w, rows):
        return jnp.pad(w, [(0, 0)] * (w.ndim - 2) + [(0, rows - w.shape[-2]), (0, 0)])

    ctx_groups = N_CTX_SEQ // SEQ_PER_GROUP
    s0_ctx = jnp.zeros((ctx_groups, HEAD, V_TILES, SUBLANES, LANES), F32)
    ctx_states = []
    for l in range(DEPTH):
        mod_l = mod[l]
        p = _inproj(x, mod_l, norm1_g[l][None], pad_cols(w_in[l]).astype(BF16))

        outs = _prep(p, pad_cols(mu_shift[l])[:, :COL_U], w0[l][:, perm], a0[l][:, perm],
                     pad_rows(w2[l][..., perm], LANES).astype(BF16),
                     pad_rows(a2[l][..., perm], LANES).astype(BF16),
                     pad_rows(g2[l][:, perm], 2 * LANES).astype(BF16),
                     k_k[l][perm][None], k_a[l][perm][None], r_k[l].reshape(-1)[perm][None],
                     n_rows=N_ROWS, ctx_rows=CTX_ROWS, image_rows=LAT_LEN)
        t_arrays, (gate, v_nat, bsum) = outs[:9], outs[9:]
        gnw_t, gnb_t = _head_param_tile(gn_w[l]), _head_param_tile(gn_b[l])

        z_ctx, sf0, sf1 = _rwkv_stream(t_arrays, CTX_STREAM, s0_ctx, s0_ctx, gnw_t, gnb_t, N_ROWS)
        s_lat = state_rwkv[:, l]
        z_lat, _, _ = _rwkv_stream(t_arrays, LAT_STREAM, _state_to_chain(s_lat[None, :, 0]),
                                   _state_to_chain(s_lat[None, :, 1]), gnw_t, gnb_t, N_ROWS)
        ctx_states.append(jnp.stack([_state_from_chain(sf0), _state_from_chain(sf1)], axis=1))

        ob = _gmlp(p, gmlp_ln_g[l], gmlp_ln_b[l], w_spatial[l].astype(BF16),
                   jnp.broadcast_to(b_spatial[l][:, :, None], (N_GROUPS_B, CHUNK, LANES)))
        w_out_l = jnp.concatenate([w_out[l][:D_A][perm], w_out[l][D_A:]], axis=0).astype(BF16)
        x = _outproj(x, z_ctx, z_lat, v_nat, bsum, gate, ob, mod_l, w_out_l)
        x = _ffn(x, mod_l, norm2_g[l][None], w_ffn_gate[l].astype(BF16), w_ffn_up[l].astype(BF16),
                 w_ffn_down[l].astype(BF16))

    y_prompt = _final_norm(x, final_norm_g[None], 0, CTX_ROWS).reshape(N_CTX_SEQ, CTX_LEN, D_MODEL)
    y_sample = _final_norm(x, final_norm_g[None], CTX_ROWS, N_ROWS - CTX_ROWS).reshape(N_LAT, LAT_LEN, D_MODEL)
    new_state = jnp.stack(ctx_states, axis=1)
    return (y_prompt, y_sample, new_state)
```

```python
import functools
import math

import jax
import jax.numpy as jnp
from jax import lax
from jax.experimental import pallas as pl
from jax.experimental.pallas import tpu as pltpu

F32 = jnp.float32
BF16 = jnp.bfloat16

D_MODEL = 2048
DEPTH = 4
N_CTX_SEQ = 16
CTX_LEN = 256
N_LAT = 4
LAT_LEN = 4096
GRID_W = 64
D_A = 1024
HEAD = 64
N_HEADS = 16
D_B = 1024
N_GROUPS_B = 8
CHUNK = 128
DECAY_LORA = 64
A_LORA = 64
GATE_LORA = 160
D_FF = 5632
RMS_EPS = 1e-6
GN_EPS = HEAD * 1e-5
LN_EPS = 1e-5

GROUP_ROWS = 4096
CTX_ROWS = N_CTX_SEQ * CTX_LEN
N_ROWS = CTX_ROWS + N_LAT * LAT_LEN
LANES = 128
SUBLANES = 8
SEQ_PER_GROUP = 4
HALF = HEAD // 2

COL_R, COL_K, COL_V = 0, D_A, 2 * D_A
COL_LORA = 3 * D_A
LORA_W = 512
COL_U = COL_LORA + LORA_W
COL_G = COL_U + D_B
P_PAD = COL_G + D_B

VMEM_LIMIT = 48 * 1024 * 1024

CTX_STREAM = (0, N_CTX_SEQ, CTX_LEN)
LAT_STREAM = (CTX_ROWS, N_LAT, LAT_LEN)


def _cparams(sem):
    return pltpu.CompilerParams(dimension_semantics=sem, vmem_limit_bytes=VMEM_LIMIT)


def _silu(x):
    return x * jax.nn.sigmoid(x)


def _gelu_tanh(x):
    return 0.5 * x * (1.0 + jnp.tanh(math.sqrt(2.0 / math.pi) * (x + 0.044715 * (x * x * x))))


def _modnorm(x, g, scale, shift):
    y = x * lax.rsqrt(jnp.mean(x * x, axis=-1, keepdims=True) + RMS_EPS)
    return (y * g) * (1.0 + scale) + shift


def _mod_kernel(c_ref, w_ref, b_ref, o_ref):
    a = _silu(c_ref[...]).astype(BF16)
    o_ref[0] = jnp.dot(a, w_ref[0].astype(BF16), preferred_element_type=F32) + b_ref[0]


def _modulation(c_all, w_mod, b_mod):
    tn = 1024
    n6 = 6 * D_MODEL
    return pl.pallas_call(
        _mod_kernel,
        grid=(DEPTH, n6 // tn),
        in_specs=[
            pl.BlockSpec((SUBLANES, D_MODEL), lambda l, j: (0, 0)),
            pl.BlockSpec((1, D_MODEL, tn), lambda l, j: (l, 0, j)),
            pl.BlockSpec((1, 1, tn), lambda l, j: (l, 0, j)),
        ],
        out_specs=pl.BlockSpec((1, SUBLANES, tn), lambda l, j: (l, 0, j)),
        out_shape=jax.ShapeDtypeStruct((DEPTH, SUBLANES, n6), F32),
        compiler_params=_cparams(("arbitrary", "arbitrary")),
        name="modulation",
    )(c_all, w_mod, b_mod.reshape(DEPTH, 1, n6))


def _inproj_kernel(x_ref, mod_ref, g_ref, w_ref, o_ref, h_ref):
    @pl.when(pl.program_id(1) == 0)
    def _():
        m = mod_ref[0]
        h_ref[...] = _modnorm(x_ref[...], g_ref[...], m[1:2], m[0:1]).astype(BF16)

    o_ref[...] = jnp.dot(h_ref[...], w_ref[0], preferred_element_type=F32)


def _inproj(x, mod_l, g, w, l):
    tm, tn = 1024, 512
    per_group = GROUP_ROWS // tm
    return pl.pallas_call(
        _inproj_kernel,
        grid=(N_ROWS // tm, P_PAD // tn),
        in_specs=[
            pl.BlockSpec((tm, D_MODEL), lambda i, j: (i, 0)),
            pl.BlockSpec((1, 6, D_MODEL), lambda i, j: (i // per_group, 0, 0)),
            pl.BlockSpec((1, D_MODEL), lambda i, j: (0, 0)),
            pl.BlockSpec((1, D_MODEL, tn), lambda i, j: (l, 0, j)),
        ],
        out_specs=pl.BlockSpec((tm, tn), lambda i, j: (i, j)),
        out_shape=jax.ShapeDtypeStruct((N_ROWS, P_PAD), F32),
        scratch_shapes=[pltpu.VMEM((tm, D_MODEL), BF16)],
        compiler_params=_cparams(("arbitrary", "arbitrary")),
        name="inproj",
    )(x, mod_l, g, w)


PREP_TM = 256
HALO = GRID_W


def _shift(x, top, bot, mu, row, is_ctx, has_top, has_bot):
    tm = x.shape[0]
    period = jnp.where(is_ctx, CTX_LEN, GRID_W)
    pos = row & (period - 1)
    local = lax.broadcasted_iota(jnp.int32, (tm, 1), 0)
    prev = jnp.where(local == 0, top[HALO - 1:HALO], pltpu.roll(x, 1, axis=0))
    nxt = jnp.where(local == tm - 1, bot[0:1], pltpu.roll(x, tm - 1, axis=0))
    prev = jnp.where(pos == 0, 0.0, prev)
    nxt = jnp.where(pos == period - 1, 0.0, nxt)
    up = jnp.concatenate([jnp.where(has_top, top, 0.0), x[: tm - HALO]], axis=0)
    down = jnp.concatenate([x[HALO:], jnp.where(has_bot, bot, 0.0)], axis=0)
    mu_up = jnp.where(is_ctx, 0.0, mu[2:3])
    mu_down = jnp.where(is_ctx, 0.0, mu[3:4])
    centre = 1.0 - mu[0:1] - mu[1:2] - mu_up - mu_down
    return x * centre + mu[0:1] * prev + mu[1:2] * nxt + mu_up * up + mu_down * down


def _head_sum(x):
    s = x[:, 0:LANES]
    for j in range(1, D_A // LANES):
        s = s + x[:, j * LANES:(j + 1) * LANES]
    for sh in (N_HEADS, 2 * N_HEADS, 4 * N_HEADS):
        s = s + pltpu.roll(s, sh, axis=1)
    return s


def _tile_heads(s):
    return jnp.concatenate([s] * (D_A // LANES), axis=1)


def _prep_kernel(r_ref, rt_ref, rb_ref, k_ref, kt_ref, kb_ref, v_ref, vt_ref, vb_ref,
                 l_ref, lt_ref, lb_ref, mu_ref, w0_ref, a0_ref, w2_ref, a2_ref, g2_ref,
                 kk_ref, ka_ref, rk_ref,
                 nkk_o, dec0_o, dec1_o, b0_o, b1_o, kd0_o, kd1_o, r_o, v_o, gate_o, vnat_o, bsum_o,
                 *, ctx_tiles, tiles_per_image):
    i = pl.program_id(0)
    is_ctx = i < ctx_tiles
    img_tile = (i - ctx_tiles) % tiles_per_image
    has_top = img_tile != 0
    has_bot = img_tile != tiles_per_image - 1
    row = i * PREP_TM + lax.broadcasted_iota(jnp.int32, (PREP_TM, 1), 0)
    sh = functools.partial(_shift, row=row, is_ctx=is_ctx, has_top=has_top, has_bot=has_bot)

    r = sh(r_ref[...], rt_ref[...], rb_ref[...], mu_ref[:, COL_R:COL_R + D_A])
    k = sh(k_ref[...], kt_ref[...], kb_ref[...], mu_ref[:, COL_K:COL_K + D_A])
    v = sh(v_ref[...], vt_ref[...], vb_ref[...], mu_ref[:, COL_V:COL_V + D_A])
    lo = sh(l_ref[...], lt_ref[...], lb_ref[...], mu_ref[:, COL_LORA:COL_LORA + LORA_W])
    wt = jnp.tanh(lo[:, 0:LANES]).astype(BF16)
    ad = lo[:, LANES:2 * LANES].astype(BF16)
    gs = jax.nn.sigmoid(lo[:, 2 * LANES:]).astype(BF16)

    kk = k * kk_ref[...]
    inv = lax.rsqrt(jnp.maximum(_head_sum(kk * kk), 1e-24))
    kk = kk * _tile_heads(inv)
    r_o[...] = r.T
    v_o[...] = v.T
    vnat_o[...] = v
    nkk_o[...] = (-kk).T
    gate_o[...] = jnp.dot(gs, g2_ref[...], preferred_element_type=F32)
    ka = ka_ref[...]
    rrk = r * rk_ref[...]
    bsum = None
    for d, (dec_o, b_o, kd_o) in enumerate(((dec0_o, b0_o, kd0_o), (dec1_o, b1_o, kd1_o))):
        wl = w0_ref[d:d + 1] + jnp.dot(wt, w2_ref[d], preferred_element_type=F32)
        dec_o[...] = jnp.exp(-(math.exp(-0.5) * jax.nn.sigmoid(wl))).T
        a = jax.nn.sigmoid(a0_ref[d:d + 1] + jnp.dot(ad, a2_ref[d], preferred_element_type=F32))
        kd = k * (1.0 + (a - 1.0) * ka)
        kd_o[...] = kd.T
        b_o[...] = (kk * a).T
        bonus = _head_sum(rrk * kd)
        bsum = bonus if bsum is None else bsum + bonus
    bsum_o[...] = bsum


def _prep(p, mu, w0, a0, w2, a2, g2, k_k, k_a, r_k, *, n_rows, ctx_rows, image_rows):
    tm = PREP_TM
    hb = tm // HALO
    n_halo = n_rows // HALO

    def main(col0, width):
        return pl.BlockSpec((tm, width), lambda i: (i, col0 // width))

    def top(col0, width):
        return pl.BlockSpec((HALO, width), lambda i: (jnp.maximum(i * hb - 1, 0), col0 // width))

    def bot(col0, width):
        return pl.BlockSpec((HALO, width), lambda i: (jnp.minimum(i * hb + hb, n_halo - 1), col0 // width))

    def full(a):
        nd = a.ndim
        return pl.BlockSpec(a.shape, lambda i: (0,) * nd)

    in_specs = []
    for col0 in (COL_R, COL_K, COL_V):
        in_specs += [main(col0, D_A), top(col0, D_A), bot(col0, D_A)]
    in_specs += [main(COL_LORA, LORA_W), top(COL_LORA, LORA_W), bot(COL_LORA, LORA_W)]
    params = (mu, w0, a0, w2, a2, g2, k_k, k_a, r_k)
    in_specs += [full(a) for a in params]
    t_spec = pl.BlockSpec((D_A, tm), lambda i: (0, i))
    n_spec = pl.BlockSpec((tm, D_A), lambda i: (i, 0))
    t_shape = jax.ShapeDtypeStruct((D_A, n_rows), F32)
    n_shape = jax.ShapeDtypeStruct((n_rows, D_A), F32)
    return pl.pallas_call(
        functools.partial(_prep_kernel, ctx_tiles=ctx_rows // tm, tiles_per_image=image_rows // tm),
        grid=(n_rows // tm,),
        in_specs=in_specs,
        out_specs=[t_spec] * 9 + [n_spec, n_spec, pl.BlockSpec((tm, LANES), lambda i: (i, 0))],
        out_shape=[t_shape] * 9 + [n_shape, n_shape, jax.ShapeDtypeStruct((n_rows, LANES), F32)],
        compiler_params=_cparams(("arbitrary",)),
        name="rwkv_prep",
    )(p, p, p, p, p, p, p, p, p, p, p, p, *params)


RELAYOUT_T = 256
FROM_CHAIN_T = 128
K_PER_BLOCK = LANES // N_HEADS


def _seq_col_block(stream, g, b, tb):
    row0, _, t_len = stream
    return (row0 + (g * SEQ_PER_GROUP + b) * t_len) // RELAYOUT_T + tb


def _to_chain_kernel(*refs, n_arrays):
    ins, outs = refs[:SEQ_PER_GROUP * n_arrays], refs[SEQ_PER_GROUP * n_arrays:]
    for a in range(n_arrays):
        blks = [ins[SEQ_PER_GROUP * a + b][...] for b in range(SEQ_PER_GROUP)]
        for kk in range(K_PER_BLOCK):
            slabs = [blk[kk * N_HEADS:(kk + 1) * N_HEADS] for blk in blks]
            outs[a][0, kk] = jnp.concatenate(slabs + slabs, axis=0).T


def _to_chain(arrays, stream):
    _, n_seq, t_len = stream
    groups = n_seq // SEQ_PER_GROUP
    n = len(arrays)
    in_specs, operands = [], []
    for a in arrays:
        for b in range(SEQ_PER_GROUP):
            in_specs.append(pl.BlockSpec(
                (LANES, RELAYOUT_T),
                functools.partial(lambda b, g, tb, kb: (kb, _seq_col_block(stream, g, b, tb)), b)))
            operands.append(a)
    out_spec = pl.BlockSpec((1, K_PER_BLOCK, RELAYOUT_T, LANES), lambda g, tb, kb: (g, kb, tb, 0))
    out_shape = jax.ShapeDtypeStruct((groups, HEAD, t_len, LANES), F32)
    return pl.pallas_call(
        functools.partial(_to_chain_kernel, n_arrays=n),
        grid=(groups, t_len // RELAYOUT_T, HEAD // K_PER_BLOCK),
        in_specs=in_specs,
        out_specs=[out_spec] * n,
        out_shape=[out_shape] * n,
        compiler_params=_cparams(("arbitrary", "arbitrary", "arbitrary")),
        name="to_chain",
    )(*operands)


def _value_to_chain_kernel(v0_ref, v1_ref, v2_ref, v3_ref, o_ref):
    blks = [r[...] for r in (v0_ref, v1_ref, v2_ref, v3_ref)]
    for vl in range(HALF):
        slabs = [blk[(vh * HALF + vl) * N_HEADS:(vh * HALF + vl + 1) * N_HEADS]
                 for vh in range(2) for blk in blks]
        o_ref[0, pl.ds(vl, RELAYOUT_T, stride=HALF), :] = jnp.concatenate(slabs, axis=0).T


def _value_to_chain(v_t, stream):
    _, n_seq, t_len = stream
    groups = n_seq // SEQ_PER_GROUP
    in_specs = [pl.BlockSpec((D_A, RELAYOUT_T),
                             functools.partial(lambda b, g, tb: (0, _seq_col_block(stream, g, b, tb)), b))
                for b in range(SEQ_PER_GROUP)]
    return pl.pallas_call(
        _value_to_chain_kernel,
        grid=(groups, t_len // RELAYOUT_T),
        in_specs=in_specs,
        out_specs=pl.BlockSpec((1, RELAYOUT_T * HALF, LANES), lambda g, tb: (g, tb, 0)),
        out_shape=jax.ShapeDtypeStruct((groups, t_len * HALF, LANES), F32),
        compiler_params=_cparams(("arbitrary", "arbitrary")),
        name="value_to_chain",
    )(v_t, v_t, v_t, v_t)


def _from_chain_kernel(z0_ref, z1_ref, o_ref, t_ref):
    b = pl.program_id(2)

    @pl.when(b == 0)
    def _():
        for vl in range(HALF):
            rows = pl.ds(vl, FROM_CHAIN_T, stride=HALF)
            t_ref[vl] = (z0_ref[0, rows, :] + z1_ref[0, rows, :]).T

    for vl in range(HALF):
        for vh in range(2):
            src = pl.ds(pl.multiple_of(vh * (LANES // 2) + b * N_HEADS, N_HEADS), N_HEADS)
            o_ref[(vh * HALF + vl) * N_HEADS:(vh * HALF + vl + 1) * N_HEADS, :] = t_ref[vl, src, :]


def _from_chain(z0, z1, stream, n_rows):
    _, n_seq, t_len = stream
    groups = n_seq // SEQ_PER_GROUP
    nt = t_len // FROM_CHAIN_T
    z_spec = pl.BlockSpec((1, FROM_CHAIN_T * HALF, LANES), lambda g, tb, b: (g, tb, 0))
    return pl.pallas_call(
        _from_chain_kernel,
        grid=(groups, nt, SEQ_PER_GROUP),
        in_specs=[z_spec, z_spec],
        out_specs=pl.BlockSpec((D_A, FROM_CHAIN_T), lambda g, tb, b: (0, (g * SEQ_PER_GROUP + b) * nt + tb)),
        out_shape=jax.ShapeDtypeStruct((D_A, n_seq * t_len), F32),
        scratch_shapes=[pltpu.VMEM((HALF, LANES, FROM_CHAIN_T), F32)],
        compiler_params=_cparams(("arbitrary", "arbitrary", "arbitrary")),
        name="from_chain",
    )(z0, z1)


SCAN_TT = 64
V_TILES = HALF // SUBLANES
ACC_PARTS = 2


def _scan_kernel(a_ref, w_ref, b_ref, kd_ref, r_ref, vv_ref, s0_ref, gnw_ref, gnb_ref,
                 z_ref, sfin_ref, s_ref, *, reverse):
    @pl.when(pl.program_id(1) == 0)
    def _():
        s_ref[...] = s0_ref[0]

    gnw = gnw_ref[...]
    gnb = gnb_ref[...]

    def block_row(i):
        return SCAN_TT - 1 - i if reverse else i

    def row(ref, k, ii):
        return ref[0, k, pl.ds(ii, 1), :]

    def value_rows(ii):
        start = ii * HALF
        return pl.ds(start if isinstance(start, int) else pl.multiple_of(start, HALF), HALF)

    def accumulate(parts, k, term):
        j = k % ACC_PARTS
        parts[j] = term if parts[j] is None else parts[j] + term

    def total(parts):
        out = parts[0]
        for p in parts[1:]:
            out = out + p
        return out

    def group_norm_store(y, ii):
        def head_mean(x):
            part = jnp.sum(jnp.sum(x, axis=0), axis=0, keepdims=True)
            return (part + pltpu.roll(part, LANES // 2, axis=1)) * (1.0 / HEAD)

        yc = y - head_mean(y)
        z = yc * lax.rsqrt(head_mean(yc * yc) + GN_EPS) * gnw + gnb
        z_ref[0, value_rows(ii), :] = z.reshape(HALF, LANES)

    first = block_row(0)
    parts = [None] * ACC_PARTS
    for k in range(HEAD):
        accumulate(parts, k, s_ref[k] * row(a_ref, k, first))

    def step(i, carry):
        sa, y_prev = carry
        ii = block_row(i)
        ii_next = block_row(jnp.minimum(i + 1, SCAN_TT - 1))
        group_norm_store(y_prev, block_row(jnp.maximum(i - 1, 0)))
        vv = vv_ref[0, value_rows(ii), :].reshape(V_TILES, SUBLANES, LANES)
        y_parts = [None] * ACC_PARTS
        sa_parts = [None] * ACC_PARTS
        for k in range(HEAD):
            s = s_ref[k] * row(w_ref, k, ii) + sa * row(b_ref, k, ii) + vv * row(kd_ref, k, ii)
            s_ref[k] = s
            accumulate(y_parts, k, s * row(r_ref, k, ii))
            accumulate(sa_parts, k, s * row(a_ref, k, ii_next))
        return total(sa_parts), total(y_parts)

    zero = jnp.zeros((V_TILES, SUBLANES, LANES), F32)
    _, y_last = lax.fori_loop(0, SCAN_TT, step, (total(parts), zero), unroll=4)
    group_norm_store(y_last, block_row(SCAN_TT - 1))

    @pl.when(pl.program_id(1) == pl.num_programs(1) - 1)
    def _():
        sfin_ref[0] = s_ref[...]


def _scan(a, w, b, kd, r, vv, s0, gnw, gnb, *, reverse):
    g, t = a.shape[0], a.shape[2]
    tt = SCAN_TT
    nt = t // tt
    tblk = (lambda ti: nt - 1 - ti) if reverse else (lambda ti: ti)
    kspec = pl.BlockSpec((1, HEAD, tt, LANES), lambda gi, ti: (gi, 0, tblk(ti), 0))
    vspec = pl.BlockSpec((1, tt * HALF, LANES), lambda gi, ti: (gi, tblk(ti), 0))
    sspec = pl.BlockSpec((1, HEAD, V_TILES, SUBLANES, LANES), lambda gi, ti: (gi, 0, 0, 0, 0))
    pspec = pl.BlockSpec((V_TILES, SUBLANES, LANES), lambda gi, ti: (0, 0, 0))
    return pl.pallas_call(
        functools.partial(_scan_kernel, reverse=reverse),
        grid=(g, nt),
        in_specs=[kspec] * 5 + [vspec, sspec, pspec, pspec],
        out_specs=[vspec, sspec],
        out_shape=[jax.ShapeDtypeStruct((g, t * HALF, LANES), F32),
                   jax.ShapeDtypeStruct((g, HEAD, V_TILES, SUBLANES, LANES), F32)],
        scratch_shapes=[pltpu.VMEM((HEAD, V_TILES, SUBLANES, LANES), F32)],
        compiler_params=_cparams(("arbitrary", "arbitrary")),
        name="delta_scan_bwd" if reverse else "delta_scan_fwd",
    )(a, w, b, kd, r, vv, s0, gnw, gnb)


def _gmlp_kernel(u_ref, v_ref, lng_ref, lnb_ref, ws_ref, bs_ref, o_ref):
    g = pl.program_id(1)
    v = _gelu_tanh(v_ref[...])
    mean = jnp.mean(v, axis=-1, keepdims=True)
    vc = v - mean
    var = jnp.mean(vc * vc, axis=-1, keepdims=True)
    vn = (vc * lax.rsqrt(var + LN_EPS) * lng_ref[pl.ds(g, 1), :] + lnb_ref[pl.ds(g, 1), :]).astype(BF16)
    u = _gelu_tanh(u_ref[...])
    ws = ws_ref[0]
    bs = bs_ref[0]
    for c in range(u.shape[0] // CHUNK):
        rows = slice(c * CHUNK, (c + 1) * CHUNK)
        s = jnp.dot(ws, vn[rows], preferred_element_type=F32) + bs
        o_ref[rows, :] = (u[rows] * s).astype(o_ref.dtype)


def _gmlp(p, ln_g, ln_b, w_sp, b_sp):
    tm = 1024
    ub, gb = COL_U // LANES, COL_G // LANES
    return pl.pallas_call(
        _gmlp_kernel,
        grid=(N_ROWS // tm, N_GROUPS_B),
        in_specs=[
            pl.BlockSpec((tm, LANES), lambda i, g: (i, ub + g)),
            pl.BlockSpec((tm, LANES), lambda i, g: (i, gb + g)),
            pl.BlockSpec((N_GROUPS_B, LANES), lambda i, g: (0, 0)),
            pl.BlockSpec((N_GROUPS_B, LANES), lambda i, g: (0, 0)),
            pl.BlockSpec((1, CHUNK, CHUNK), lambda i, g: (g, 0, 0)),
            pl.BlockSpec((1, CHUNK, LANES), lambda i, g: (g, 0, 0)),
        ],
        out_specs=pl.BlockSpec((tm, LANES), lambda i, g: (i, g)),
        out_shape=jax.ShapeDtypeStruct((N_ROWS, D_B), BF16),
        compiler_params=_cparams(("arbitrary", "arbitrary")),
        name="chunk_gmlp",
    )(p, p, ln_g, ln_b, w_sp, b_sp)


def _outproj_kernel(x_ref, zc_ref, zl_ref, v_ref, bsum_ref, gate_ref, ob_ref, mod_ref, w_ref, o_ref, a_ref,
                    *, ctx_tiles):
    i = pl.program_id(0)

    @pl.when(pl.program_id(1) == 0)
    def _():
        def fill(z_ref):
            oa = (z_ref[...].T + _tile_heads(bsum_ref[...]) * v_ref[...]) * gate_ref[...]
            a_ref[:, :D_A] = oa.astype(BF16)

        pl.when(i < ctx_tiles)(lambda: fill(zc_ref))
        pl.when(i >= ctx_tiles)(lambda: fill(zl_ref))
        a_ref[:, D_A:] = ob_ref[...]

    tn = o_ref.shape[1]
    j = pl.program_id(1)
    g1 = mod_ref[0, 2:3, pl.ds(pl.multiple_of(j * tn, tn), tn)]
    o_ref[...] = x_ref[...] + g1 * jnp.dot(a_ref[...], w_ref[0], preferred_element_type=F32)


def _outproj(x, z_ctx, z_lat, v, bsum, gate, ob, mod_l, w, l):
    tm, tn = 512, 1024
    per_group = GROUP_ROWS // tm
    ctx_tiles = CTX_ROWS // tm
    n_lat_tiles = (N_ROWS - CTX_ROWS) // tm
    half = pl.BlockSpec((tm, D_A), lambda i, j: (i, 0))
    return pl.pallas_call(
        functools.partial(_outproj_kernel, ctx_tiles=ctx_tiles),
        grid=(N_ROWS // tm, D_MODEL // tn),
        in_specs=[
            pl.BlockSpec((tm, tn), lambda i, j: (i, j)),
            pl.BlockSpec((D_A, tm), lambda i, j: (0, jnp.minimum(i, ctx_tiles - 1))),
            pl.BlockSpec((D_A, tm), lambda i, j: (0, jnp.clip(i - ctx_tiles, 0, n_lat_tiles - 1))),
            half,
            pl.BlockSpec((tm, LANES), lambda i, j: (i, 0)),
            half, half,
            pl.BlockSpec((1, 6, D_MODEL), lambda i, j: (i // per_group, 0, 0)),
            pl.BlockSpec((1, D_MODEL, tn), lambda i, j: (l, 0, j)),
        ],
        out_specs=pl.BlockSpec((tm, tn), lambda i, j: (i, j)),
        out_shape=jax.ShapeDtypeStruct((N_ROWS, D_MODEL), F32),
        scratch_shapes=[pltpu.VMEM((tm, D_MODEL), BF16)],
        compiler_params=_cparams(("arbitrary", "arbitrary")),
        name="outproj",
    )(x, z_ctx, z_lat, v, bsum, gate, ob, mod_l, w)


def _ffn_kernel(x_ref, mod_ref, g_ref, wg_ref, wu_ref, wd_ref, o_ref, h_ref, acc_ref):
    f = pl.program_id(1)

    @pl.when(f == 0)
    def _():
        m = mod_ref[0]
        h_ref[...] = _modnorm(x_ref[...], g_ref[...], m[4:5], m[3:4]).astype(BF16)
        acc_ref[...] = jnp.zeros_like(acc_ref)

    h = h_ref[...]
    gate = jnp.dot(h, wg_ref[0], preferred_element_type=F32)
    up = jnp.dot(h, wu_ref[0], preferred_element_type=F32)
    act = (_silu(gate) * up).astype(BF16)
    acc_ref[...] += jnp.dot(act, wd_ref[0], preferred_element_type=F32)

    @pl.when(f == pl.num_programs(1) - 1)
    def _():
        o_ref[...] = x_ref[...] + mod_ref[0, 5:6] * acc_ref[...]


def _ffn(x, mod_l, g, wg, wu, wd, l):
    tm, tf = 512, 512
    per_group = GROUP_ROWS // tm
    return pl.pallas_call(
        _ffn_kernel,
        grid=(N_ROWS // tm, D_FF // tf),
        in_specs=[
            pl.BlockSpec((tm, D_MODEL), lambda i, f: (i, 0)),
            pl.BlockSpec((1, 6, D_MODEL), lambda i, f: (i // per_group, 0, 0)),
            pl.BlockSpec((1, D_MODEL), lambda i, f: (0, 0)),
            pl.BlockSpec((1, D_MODEL, tf), lambda i, f: (l, 0, f)),
            pl.BlockSpec((1, D_MODEL, tf), lambda i, f: (l, 0, f)),
            pl.BlockSpec((1, tf, D_MODEL), lambda i, f: (l, f, 0)),
        ],
        out_specs=pl.BlockSpec((tm, D_MODEL), lambda i, f: (i, 0)),
        out_shape=jax.ShapeDtypeStruct((N_ROWS, D_MODEL), F32),
        scratch_shapes=[pltpu.VMEM((tm, D_MODEL), BF16), pltpu.VMEM((tm, D_MODEL), F32)],
        compiler_params=_cparams(("arbitrary", "arbitrary")),
        name="ffn",
    )(x, mod_l, g, wg, wu, wd)


def _final_norm_kernel(x_ref, g_ref, o_ref):
    x = x_ref[...]
    o_ref[...] = x * lax.rsqrt(jnp.mean(x * x, axis=-1, keepdims=True) + RMS_EPS) * g_ref[...]


def _final_norm(x, g, row0, rows):
    tm = 512
    return pl.pallas_call(
        _final_norm_kernel,
        grid=(rows // tm,),
        in_specs=[pl.BlockSpec((tm, D_MODEL), lambda i: (row0 // tm + i, 0)),
                  pl.BlockSpec((1, D_MODEL), lambda i: (0, 0))],
        out_specs=pl.BlockSpec((tm, D_MODEL), lambda i: (i, 0)),
        out_shape=jax.ShapeDtypeStruct((rows, D_MODEL), F32),
        compiler_params=_cparams(("arbitrary",)),
        name="final_norm",
    )(x, g)


def _head_param_tile(p):
    t = p.T.reshape(2, HALF, 1, N_HEADS)
    t = jnp.broadcast_to(t, (2, HALF, SEQ_PER_GROUP, N_HEADS))
    return t.transpose(1, 0, 2, 3).reshape(V_TILES, SUBLANES, LANES)


def _state_to_chain(s):
    g = s.shape[0]
    s = s.reshape(g, SEQ_PER_GROUP, N_HEADS, 2, HALF, HEAD)
    return s.transpose(0, 5, 4, 3, 1, 2).reshape(g, HEAD, V_TILES, SUBLANES, LANES)


def _state_from_chain(s):
    g = s.shape[0]
    s = s.reshape(g, HEAD, HALF, 2, SEQ_PER_GROUP, N_HEADS)
    return s.transpose(0, 4, 5, 3, 2, 1).reshape(g * SEQ_PER_GROUP, N_HEADS, HEAD, HEAD)


def _rwkv_stream(t_arrays, stream, s0_fwd, s0_bwd, gnw_t, gnb_t, n_rows):
    nkk_t, dec0_t, dec1_t, b0_t, b1_t, kd0_t, kd1_t, r_t, v_t = t_arrays
    nkk_c, dec0_c, dec1_c, b0_c, b1_c, kd0_c, kd1_c, r_c = _to_chain(
        (nkk_t, dec0_t, dec1_t, b0_t, b1_t, kd0_t, kd1_t, r_t), stream)
    vv_c = _value_to_chain(v_t, stream)
    z0, sf0 = _scan(nkk_c, dec0_c, b0_c, kd0_c, r_c, vv_c, s0_fwd, gnw_t, gnb_t, reverse=False)
    z1, sf1 = _scan(nkk_c, dec1_c, b1_c, kd1_c, r_c, vv_c, s0_bwd, gnw_t, gnb_t, reverse=True)
    return _from_chain(z0, z1, stream, n_rows), sf0, sf1


def _index_major(n=N_HEADS, m=HEAD):
    return jnp.arange(n * m).reshape(n, m).T.reshape(-1)


def kernel(x_prompt, x_sample, state_rwkv, c, c_ctx, w_mod, b_mod, norm1_g, w_in, mu_shift, w0, w2, a0, a2,
           g2, k_k, k_a, r_k, gn_w, gn_b, gmlp_ln_g, gmlp_ln_b, w_spatial, b_spatial, w_out, norm2_g,
           w_ffn_gate, w_ffn_up, w_ffn_down, final_norm_g):
    x = jnp.concatenate([x_prompt.reshape(CTX_ROWS, D_MODEL), x_sample.reshape(N_LAT * LAT_LEN, D_MODEL)], axis=0)
    c_all = jnp.concatenate([c_ctx[None], c, jnp.zeros((SUBLANES - 1 - N_LAT, D_MODEL), F32)], axis=0)
    mod = _modulation(c_all, w_mod, b_mod).reshape(DEPTH, SUBLANES, 6, D_MODEL)
    perm = _index_major()

    def pad_cols(cols):
        c0 = 3 * D_A
        r, k, v = (cols[..., j * D_A:(j + 1) * D_A][..., perm] for j in range(3))
        wd = cols[..., c0:c0 + DECAY_LORA]
        ad = cols[..., c0 + DECAY_LORA:c0 + DECAY_LORA + A_LORA]
        gd = cols[..., c0 + DECAY_LORA + A_LORA:c0 + DECAY_LORA + A_LORA + GATE_LORA]
        ug = cols[..., c0 + DECAY_LORA + A_LORA + GATE_LORA:]
        z = lambda n: jnp.zeros(cols.shape[:-1] + (n,), cols.dtype)
        return jnp.concatenate([r, k, v, wd, z(LANES - DECAY_LORA), ad, z(LANES - A_LORA),
                                gd, z(2 * LANES - GATE_LORA), ug], axis=-1)

    def pad_rows(w, rows):
        return jnp.pad(w, [(0, 0)] * (w.ndim - 2) + [(0, rows - w.shape[-2]), (0, 0)])

    w_in_p = pad_cols(w_in).astype(BF16)
    mu_p = pad_cols(mu_shift)[..., :COL_U]
    w0_p, a0_p = w0[..., perm], a0[..., perm]
    w2_p = pad_rows(w2[..., perm], LANES).astype(BF16)
    a2_p = pad_rows(a2[..., perm], LANES).astype(BF16)
    g2_p = pad_rows(g2[..., perm], 2 * LANES).astype(BF16)
    kk_p, ka_p = k_k[:, None, perm], k_a[:, None, perm]
    rk_p = r_k.reshape(DEPTH, 1, D_A)[..., perm]
    w_out_p = jnp.concatenate([w_out[:, :D_A][:, perm], w_out[:, D_A:]], axis=1).astype(BF16)
    wg_p, wu_p, wd_p = w_ffn_gate.astype(BF16), w_ffn_up.astype(BF16), w_ffn_down.astype(BF16)
    w_sp_p = w_spatial.astype(BF16)
    b_sp_p = jnp.broadcast_to(b_spatial[..., None], (DEPTH, N_GROUPS_B, CHUNK, LANES))

    ctx_groups = N_CTX_SEQ // SEQ_PER_GROUP
    s0_ctx = jnp.zeros((ctx_groups, HEAD, V_TILES, SUBLANES, LANES), F32)
    ctx_states = []
    for l in range(DEPTH):
        mod_l = mod[l]
        p = _inproj(x, mod_l, norm1_g[l][None], w_in_p, l)

        outs = _prep(p, mu_p[l], w0_p[l], a0_p[l], w2_p[l], a2_p[l], g2_p[l], kk_p[l], ka_p[l], rk_p[l],
                     n_rows=N_ROWS, ctx_rows=CTX_ROWS, image_rows=LAT_LEN)
        t_arrays, (gate, v_nat, bsum) = outs[:9], outs[9:]
        gnw_t, gnb_t = _head_param_tile(gn_w[l]), _head_param_tile(gn_b[l])

        z_ctx, sf0, sf1 = _rwkv_stream(t_arrays, CTX_STREAM, s0_ctx, s0_ctx, gnw_t, gnb_t, N_ROWS)
        s_lat = state_rwkv[:, l]
        z_lat, _, _ = _rwkv_stream(t_arrays, LAT_STREAM, _state_to_chain(s_lat[None, :, 0]),
                                   _state_to_chain(s_lat[None, :, 1]), gnw_t, gnb_t, N_ROWS)
        ctx_states.append(jnp.stack([_state_from_chain(sf0), _state_from_chain(sf1)], axis=1))

        ob = _gmlp(p, gmlp_ln_g[l], gmlp_ln_b[l], w_sp_p[l], b_sp_p[l])
        x = _outproj(x, z_ctx, z_lat, v_nat, bsum, gate, ob, mod_l, w_out_p, l)
        x = _ffn(x, mod_l, norm2_g[l][None], wg_p, wu_p, wd_p, l)

    y_prompt = _final_norm(x, final_norm_g[None], 0, CTX_ROWS).reshape(N_CTX_SEQ, CTX_LEN, D_MODEL)
    y_sample = _final_norm(x, final_norm_g[None], CTX_ROWS, N_ROWS - CTX_ROWS).reshape(N_LAT, LAT_LEN, D_MODEL)
    new_state = jnp.stack(ctx_states, axis=1)
    return (y_prompt, y_sample, new_state)
```

```python
import functools
import math

import jax
import jax.numpy as jnp
from jax import lax
from jax.experimental import pallas as pl
from jax.experimental.pallas import tpu as pltpu

F32 = jnp.float32
BF16 = jnp.bfloat16

D_MODEL = 2048
DEPTH = 4
N_CTX_SEQ = 16
CTX_LEN = 256
N_LAT = 4
LAT_LEN = 4096
GRID_W = 64
D_A = 1024
HEAD = 64
N_HEADS = 16
D_B = 1024
N_GROUPS_B = 8
CHUNK = 128
DECAY_LORA = 64
A_LORA = 64
GATE_LORA = 160
D_FF = 5632
RMS_EPS = 1e-6
GN_EPS = HEAD * 1e-5
LN_EPS = 1e-5

GROUP_ROWS = 4096
CTX_ROWS = N_CTX_SEQ * CTX_LEN
N_ROWS = CTX_ROWS + N_LAT * LAT_LEN
LANES = 128
SUBLANES = 8
SEQ_PER_GROUP = 4
HALF = HEAD // 2

COL_R, COL_K, COL_V = 0, D_A, 2 * D_A
COL_LORA = 3 * D_A
LORA_W = 512
COL_U = COL_LORA + LORA_W
COL_G = COL_U + D_B
P_PAD = COL_G + D_B

VMEM_LIMIT = 48 * 1024 * 1024

CTX_STREAM = (0, N_CTX_SEQ, CTX_LEN)
LAT_STREAM = (CTX_ROWS, N_LAT, LAT_LEN)


def _cparams(sem):
    return pltpu.CompilerParams(dimension_semantics=sem, vmem_limit_bytes=VMEM_LIMIT)


def _silu(x):
    return x * jax.nn.sigmoid(x)


def _gelu_tanh(x):
    return 0.5 * x * (1.0 + jnp.tanh(math.sqrt(2.0 / math.pi) * (x + 0.044715 * (x * x * x))))


def _modnorm(x, g, scale, shift):
    y = x * lax.rsqrt(jnp.mean(x * x, axis=-1, keepdims=True) + RMS_EPS)
    return (y * g) * (1.0 + scale) + shift


def _mod_kernel(c_ref, w_ref, b_ref, o_ref):
    a = _silu(c_ref[...]).astype(BF16)
    o_ref[0] = jnp.dot(a, w_ref[0].astype(BF16), preferred_element_type=F32) + b_ref[0]


def _modulation(c_all, w_mod, b_mod):
    tn = 1024
    n6 = 6 * D_MODEL
    return pl.pallas_call(
        _mod_kernel,
        grid=(DEPTH, n6 // tn),
        in_specs=[
            pl.BlockSpec((SUBLANES, D_MODEL), lambda l, j: (0, 0)),
            pl.BlockSpec((1, D_MODEL, tn), lambda l, j: (l, 0, j)),
            pl.BlockSpec((1, 1, tn), lambda l, j: (l, 0, j)),
        ],
        out_specs=pl.BlockSpec((1, SUBLANES, tn), lambda l, j: (l, 0, j)),
        out_shape=jax.ShapeDtypeStruct((DEPTH, SUBLANES, n6), F32),
        compiler_params=_cparams(("arbitrary", "arbitrary")),
        name="modulation",
    )(c_all, w_mod, b_mod.reshape(DEPTH, 1, n6))


def _inproj_kernel(x_ref, mod_ref, g_ref, w_ref, o_ref, h_ref):
    @pl.when(pl.program_id(1) == 0)
    def _():
        m = mod_ref[0]
        h_ref[...] = _modnorm(x_ref[...], g_ref[...], m[1:2], m[0:1]).astype(BF16)

    o_ref[...] = jnp.dot(h_ref[...], w_ref[0], preferred_element_type=F32)


def _inproj(x, mod_l, g, w, l):
    tm, tn = 1024, 512
    per_group = GROUP_ROWS // tm
    return pl.pallas_call(
        _inproj_kernel,
        grid=(N_ROWS // tm, P_PAD // tn),
        in_specs=[
            pl.BlockSpec((tm, D_MODEL), lambda i, j: (i, 0)),
            pl.BlockSpec((1, 6, D_MODEL), lambda i, j: (i // per_group, 0, 0)),
            pl.BlockSpec((1, D_MODEL), lambda i, j: (0, 0)),
            pl.BlockSpec((1, D_MODEL, tn), lambda i, j: (l, 0, j)),
        ],
        out_specs=pl.BlockSpec((tm, tn), lambda i, j: (i, j)),
        out_shape=jax.ShapeDtypeStruct((N_ROWS, P_PAD), F32),
        scratch_shapes=[pltpu.VMEM((tm, D_MODEL), BF16)],
        compiler_params=_cparams(("arbitrary", "arbitrary")),
        name="inproj",
    )(x, mod_l, g, w)


PREP_TM = 256
HALO = GRID_W


def _shift(x, top, bot, mu, is_ctx, has_top, has_bot):
    tm = x.shape[0]
    assert tm % CTX_LEN == 0 and tm % GRID_W == 0
    period = jnp.where(is_ctx, CTX_LEN, GRID_W)
    pos = lax.broadcasted_iota(jnp.int32, (tm, 1), 0) & (period - 1)
    prev = jnp.where(pos == 0, 0.0, pltpu.roll(x, 1, axis=0))
    nxt = jnp.where(pos == period - 1, 0.0, pltpu.roll(x, tm - 1, axis=0))
    up = jnp.concatenate([jnp.where(has_top, top, 0.0), x[: tm - HALO]], axis=0)
    down = jnp.concatenate([x[HALO:], jnp.where(has_bot, bot, 0.0)], axis=0)
    mu_up = jnp.where(is_ctx, 0.0, mu[2:3])
    mu_down = jnp.where(is_ctx, 0.0, mu[3:4])
    centre = 1.0 - mu[0:1] - mu[1:2] - mu_up - mu_down
    return x * centre + mu[0:1] * prev + mu[1:2] * nxt + mu_up * up + mu_down * down


def _head_sum(x):
    s = x[:, 0:LANES]
    for j in range(1, D_A // LANES):
        s = s + x[:, j * LANES:(j + 1) * LANES]
    for sh in (N_HEADS, 2 * N_HEADS, 4 * N_HEADS):
        s = s + pltpu.roll(s, sh, axis=1)
    return s


def _tile_heads(s):
    return jnp.concatenate([s] * (D_A // LANES), axis=1)


def _prep_kernel(r_ref, rt_ref, rb_ref, k_ref, kt_ref, kb_ref, v_ref, vt_ref, vb_ref,
                 l_ref, lt_ref, lb_ref, mu_ref, w0_ref, a0_ref, w2_ref, a2_ref, g2_ref,
                 kk_ref, ka_ref, rk_ref,
                 nkk_o, dec0_o, dec1_o, b0_o, b1_o, kd0_o, kd1_o, r_o, v_o, gate_o, vnat_o, bsum_o,
                 *, ctx_tiles, tiles_per_image):
    i = pl.program_id(0)
    is_ctx = i < ctx_tiles
    img_tile = (i - ctx_tiles) % tiles_per_image
    has_top = img_tile != 0
    has_bot = img_tile != tiles_per_image - 1
    sh = functools.partial(_shift, is_ctx=is_ctx, has_top=has_top, has_bot=has_bot)

    r = sh(r_ref[...], rt_ref[...], rb_ref[...], mu_ref[:, COL_R:COL_R + D_A])
    k = sh(k_ref[...], kt_ref[...], kb_ref[...], mu_ref[:, COL_K:COL_K + D_A])
    v = sh(v_ref[...], vt_ref[...], vb_ref[...], mu_ref[:, COL_V:COL_V + D_A])
    lo = sh(l_ref[...], lt_ref[...], lb_ref[...], mu_ref[:, COL_LORA:COL_LORA + LORA_W])
    wt = jnp.tanh(lo[:, 0:LANES]).astype(BF16)
    ad = lo[:, LANES:2 * LANES].astype(BF16)
    gs = jax.nn.sigmoid(lo[:, 2 * LANES:]).astype(BF16)

    kk = k * kk_ref[...]
    inv = lax.rsqrt(jnp.maximum(_head_sum(kk * kk), 1e-24))
    kk = kk * _tile_heads(inv)
    r_o[...] = r.T
    v_o[...] = v.T
    vnat_o[...] = v
    nkk_o[...] = (-kk).T
    gate_o[...] = jnp.dot(gs, g2_ref[...], preferred_element_type=F32)
    ka = ka_ref[...]
    rrk = r * rk_ref[...]
    bsum = None
    for d, (dec_o, b_o, kd_o) in enumerate(((dec0_o, b0_o, kd0_o), (dec1_o, b1_o, kd1_o))):
        wl = w0_ref[d:d + 1] + jnp.dot(wt, w2_ref[d], preferred_element_type=F32)
        dec_o[...] = jnp.exp(-(math.exp(-0.5) * jax.nn.sigmoid(wl))).T
        a = jax.nn.sigmoid(a0_ref[d:d + 1] + jnp.dot(ad, a2_ref[d], preferred_element_type=F32))
        kd = k * (1.0 + (a - 1.0) * ka)
        kd_o[...] = kd.T
        b_o[...] = (kk * a).T
        bonus = _head_sum(rrk * kd)
        bsum = bonus if bsum is None else bsum + bonus
    bsum_o[...] = bsum


def _prep(p, mu, w0, a0, w2, a2, g2, k_k, k_a, r_k, *, n_rows, ctx_rows, image_rows):
    tm = PREP_TM
    hb = tm // HALO
    n_halo = n_rows // HALO

    def main(col0, width):
        return pl.BlockSpec((tm, width), lambda i: (i, col0 // width))

    def top(col0, width):
        return pl.BlockSpec((HALO, width), lambda i: (jnp.maximum(i * hb - 1, 0), col0 // width))

    def bot(col0, width):
        return pl.BlockSpec((HALO, width), lambda i: (jnp.minimum(i * hb + hb, n_halo - 1), col0 // width))

    def full(a):
        nd = a.ndim
        return pl.BlockSpec(a.shape, lambda i: (0,) * nd)

    in_specs = []
    for col0 in (COL_R, COL_K, COL_V):
        in_specs += [main(col0, D_A), top(col0, D_A), bot(col0, D_A)]
    in_specs += [main(COL_LORA, LORA_W), top(COL_LORA, LORA_W), bot(COL_LORA, LORA_W)]
    params = (mu, w0, a0, w2, a2, g2, k_k, k_a, r_k)
    in_specs += [full(a) for a in params]
    t_spec = pl.BlockSpec((D_A, tm), lambda i: (0, i))
    n_spec = pl.BlockSpec((tm, D_A), lambda i: (i, 0))
    t_shape = jax.ShapeDtypeStruct((D_A, n_rows), F32)
    n_shape = jax.ShapeDtypeStruct((n_rows, D_A), F32)
    return pl.pallas_call(
        functools.partial(_prep_kernel, ctx_tiles=ctx_rows // tm, tiles_per_image=image_rows // tm),
        grid=(n_rows // tm,),
        in_specs=in_specs,
        out_specs=[t_spec] * 9 + [n_spec, n_spec, pl.BlockSpec((tm, LANES), lambda i: (i, 0))],
        out_shape=[t_shape] * 9 + [n_shape, n_shape, jax.ShapeDtypeStruct((n_rows, LANES), F32)],
        compiler_params=_cparams(("arbitrary",)),
        name="rwkv_prep",
    )(p, p, p, p, p, p, p, p, p, p, p, p, *params)


RELAYOUT_T = 256
FROM_CHAIN_T = 128
K_PER_BLOCK = LANES // N_HEADS


def _seq_col_block(stream, g, b, tb):
    row0, _, t_len = stream
    return (row0 + (g * SEQ_PER_GROUP + b) * t_len) // RELAYOUT_T + tb


def _to_chain_kernel(*refs, n_arrays):
    ins, outs = refs[:SEQ_PER_GROUP * n_arrays], refs[SEQ_PER_GROUP * n_arrays:]
    for a in range(n_arrays):
        blks = [ins[SEQ_PER_GROUP * a + b][...] for b in range(SEQ_PER_GROUP)]
        for kk in range(K_PER_BLOCK):
            slabs = [blk[kk * N_HEADS:(kk + 1) * N_HEADS] for blk in blks]
            outs[a][0, kk] = jnp.concatenate(slabs + slabs, axis=0).T


def _to_chain(arrays, stream):
    _, n_seq, t_len = stream
    groups = n_seq // SEQ_PER_GROUP
    n = len(arrays)
    in_specs, operands = [], []
    for a in arrays:
        for b in range(SEQ_PER_GROUP):
            in_specs.append(pl.BlockSpec(
                (LANES, RELAYOUT_T),
                functools.partial(lambda b, g, tb, kb: (kb, _seq_col_block(stream, g, b, tb)), b)))
            operands.append(a)
    out_spec = pl.BlockSpec((1, K_PER_BLOCK, RELAYOUT_T, LANES), lambda g, tb, kb: (g, kb, tb, 0))
    out_shape = jax.ShapeDtypeStruct((groups, HEAD, t_len, LANES), F32)
    return pl.pallas_call(
        functools.partial(_to_chain_kernel, n_arrays=n),
        grid=(groups, t_len // RELAYOUT_T, HEAD // K_PER_BLOCK),
        in_specs=in_specs,
        out_specs=[out_spec] * n,
        out_shape=[out_shape] * n,
        compiler_params=_cparams(("arbitrary", "arbitrary", "arbitrary")),
        name="to_chain",
    )(*operands)


def _value_to_chain_kernel(v0_ref, v1_ref, v2_ref, v3_ref, o_ref):
    blks = [r[...] for r in (v0_ref, v1_ref, v2_ref, v3_ref)]
    for vl in range(HALF):
        slabs = [blk[(vh * HALF + vl) * N_HEADS:(vh * HALF + vl + 1) * N_HEADS]
                 for vh in range(2) for blk in blks]
        o_ref[0, pl.ds(vl, RELAYOUT_T, stride=HALF), :] = jnp.concatenate(slabs, axis=0).T


def _value_to_chain(v_t, stream):
    _, n_seq, t_len = stream
    groups = n_seq // SEQ_PER_GROUP
    in_specs = [pl.BlockSpec((D_A, RELAYOUT_T),
                             functools.partial(lambda b, g, tb: (0, _seq_col_block(stream, g, b, tb)), b))
                for b in range(SEQ_PER_GROUP)]
    return pl.pallas_call(
        _value_to_chain_kernel,
        grid=(groups, t_len // RELAYOUT_T),
        in_specs=in_specs,
        out_specs=pl.BlockSpec((1, RELAYOUT_T * HALF, LANES), lambda g, tb: (g, tb, 0)),
        out_shape=jax.ShapeDtypeStruct((groups, t_len * HALF, LANES), F32),
        compiler_params=_cparams(("arbitrary", "arbitrary")),
        name="value_to_chain",
    )(v_t, v_t, v_t, v_t)


def _from_chain_kernel(z0_ref, z1_ref, o_ref, t_ref):
    b = pl.program_id(2)

    @pl.when(b == 0)
    def _():
        for vl in range(HALF):
            rows = pl.ds(vl, FROM_CHAIN_T, stride=HALF)
            t_ref[vl] = (z0_ref[0, rows, :] + z1_ref[0, rows, :]).T

    for vl in range(HALF):
        for vh in range(2):
            src = pl.ds(pl.multiple_of(vh * (LANES // 2) + b * N_HEADS, N_HEADS), N_HEADS)
            o_ref[(vh * HALF + vl) * N_HEADS:(vh * HALF + vl + 1) * N_HEADS, :] = t_ref[vl, src, :]


def _from_chain(z0, z1, stream, n_rows):
    _, n_seq, t_len = stream
    groups = n_seq // SEQ_PER_GROUP
    nt = t_len // FROM_CHAIN_T
    z_spec = pl.BlockSpec((1, FROM_CHAIN_T * HALF, LANES), lambda g, tb, b: (g, tb, 0))
    return pl.pallas_call(
        _from_chain_kernel,
        grid=(groups, nt, SEQ_PER_GROUP),
        in_specs=[z_spec, z_spec],
        out_specs=pl.BlockSpec((D_A, FROM_CHAIN_T), lambda g, tb, b: (0, (g * SEQ_PER_GROUP + b) * nt + tb)),
        out_shape=jax.ShapeDtypeStruct((D_A, n_seq * t_len), F32),
        scratch_shapes=[pltpu.VMEM((HALF, LANES, FROM_CHAIN_T), F32)],
        compiler_params=_cparams(("arbitrary", "arbitrary", "arbitrary")),
        name="from_chain",
    )(z0, z1)


SCAN_TT = 64
V_TILES = HALF // SUBLANES
ACC_PARTS = 2


def _scan_kernel(a_ref, w_ref, b_ref, kd_ref, r_ref, vv_ref, s0_ref, gnw_ref, gnb_ref,
                 z_ref, sfin_ref, s_ref, *, reverse):
    @pl.when(pl.program_id(1) == 0)
    def _():
        s_ref[...] = s0_ref[0]

    gnw = gnw_ref[...]
    gnb = gnb_ref[...]

    def block_row(i):
        return SCAN_TT - 1 - i if reverse else i

    def row(ref, k, ii):
        return ref[0, k, pl.ds(ii, 1), :]

    def value_rows(ii):
        start = ii * HALF
        return pl.ds(start if isinstance(start, int) else pl.multiple_of(start, HALF), HALF)

    def accumulate(parts, k, term):
        j = k % ACC_PARTS
        parts[j] = term if parts[j] is None else parts[j] + term

    def total(parts):
        out = parts[0]
        for p in parts[1:]:
            out = out + p
        return out

    def group_norm_store(y, ii):
        def head_mean(x):
            part = jnp.sum(jnp.sum(x, axis=0), axis=0, keepdims=True)
            return (part + pltpu.roll(part, LANES // 2, axis=1)) * (1.0 / HEAD)

        yc = y - head_mean(y)
        z = yc * lax.rsqrt(head_mean(yc * yc) + GN_EPS) * gnw + gnb
        z_ref[0, value_rows(ii), :] = z.reshape(HALF, LANES)

    first = block_row(0)
    parts = [None] * ACC_PARTS
    for k in range(HEAD):
        accumulate(parts, k, s_ref[k] * row(a_ref, k, first))

    def step(i, carry):
        sa, y_prev = carry
        ii = block_row(i)
        ii_next = block_row(jnp.minimum(i + 1, SCAN_TT - 1))
        group_norm_store(y_prev, block_row(jnp.maximum(i - 1, 0)))
        vv = vv_ref[0, value_rows(ii), :].reshape(V_TILES, SUBLANES, LANES)
        y_parts = [None] * ACC_PARTS
        sa_parts = [None] * ACC_PARTS
        for k in range(HEAD):
            s = s_ref[k] * row(w_ref, k, ii) + sa * row(b_ref, k, ii) + vv * row(kd_ref, k, ii)
            s_ref[k] = s
            accumulate(y_parts, k, s * row(r_ref, k, ii))
            accumulate(sa_parts, k, s * row(a_ref, k, ii_next))
        return total(sa_parts), total(y_parts)

    zero = jnp.zeros((V_TILES, SUBLANES, LANES), F32)
    _, y_last = lax.fori_loop(0, SCAN_TT, step, (total(parts), zero), unroll=4)
    group_norm_store(y_last, block_row(SCAN_TT - 1))

    @pl.when(pl.program_id(1) == pl.num_programs(1) - 1)
    def _():
        sfin_ref[0] = s_ref[...]


def _scan(a, w, b, kd, r, vv, s0, gnw, gnb, *, reverse):
    g, t = a.shape[0], a.shape[2]
    tt = SCAN_TT
    nt = t // tt
    tblk = (lambda ti: nt - 1 - ti) if reverse else (lambda ti: ti)
    kspec = pl.BlockSpec((1, HEAD, tt, LANES), lambda gi, ti: (gi, 0, tblk(ti), 0))
    vspec = pl.BlockSpec((1, tt * HALF, LANES), lambda gi, ti: (gi, tblk(ti), 0))
    sspec = pl.BlockSpec((1, HEAD, V_TILES, SUBLANES, LANES), lambda gi, ti: (gi, 0, 0, 0, 0))
    pspec = pl.BlockSpec((V_TILES, SUBLANES, LANES), lambda gi, ti: (0, 0, 0))
    return pl.pallas_call(
        functools.partial(_scan_kernel, reverse=reverse),
        grid=(g, nt),
        in_specs=[kspec] * 5 + [vspec, sspec, pspec, pspec],
        out_specs=[vspec, sspec],
        out_shape=[jax.ShapeDtypeStruct((g, t * HALF, LANES), F32),
                   jax.ShapeDtypeStruct((g, HEAD, V_TILES, SUBLANES, LANES), F32)],
        scratch_shapes=[pltpu.VMEM((HEAD, V_TILES, SUBLANES, LANES), F32)],
        compiler_params=_cparams(("arbitrary", "arbitrary")),
        name="delta_scan_bwd" if reverse else "delta_scan_fwd",
    )(a, w, b, kd, r, vv, s0, gnw, gnb)


def _gmlp_kernel(u_ref, v_ref, lng_ref, lnb_ref, ws_ref, bs_ref, o_ref):
    g = pl.program_id(1)
    v = _gelu_tanh(v_ref[...])
    mean = jnp.mean(v, axis=-1, keepdims=True)
    vc = v - mean
    var = jnp.mean(vc * vc, axis=-1, keepdims=True)
    vn = (vc * lax.rsqrt(var + LN_EPS) * lng_ref[pl.ds(g, 1), :] + lnb_ref[pl.ds(g, 1), :]).astype(BF16)
    u = _gelu_tanh(u_ref[...])
    ws = ws_ref[0]
    bs = bs_ref[0]
    for c in range(u.shape[0] // CHUNK):
        rows = slice(c * CHUNK, (c + 1) * CHUNK)
        s = jnp.dot(ws, vn[rows], preferred_element_type=F32) + bs
        o_ref[rows, :] = (u[rows] * s).astype(o_ref.dtype)


def _gmlp(p, ln_g, ln_b, w_sp, b_sp):
    tm = 1024
    ub, gb = COL_U // LANES, COL_G // LANES
    return pl.pallas_call(
        _gmlp_kernel,
        grid=(N_ROWS // tm, N_GROUPS_B),
        in_specs=[
            pl.BlockSpec((tm, LANES), lambda i, g: (i, ub + g)),
            pl.BlockSpec((tm, LANES), lambda i, g: (i, gb + g)),
            pl.BlockSpec((N_GROUPS_B, LANES), lambda i, g: (0, 0)),
            pl.BlockSpec((N_GROUPS_B, LANES), lambda i, g: (0, 0)),
            pl.BlockSpec((1, CHUNK, CHUNK), lambda i, g: (g, 0, 0)),
            pl.BlockSpec((1, CHUNK, LANES), lambda i, g: (g, 0, 0)),
        ],
        out_specs=pl.BlockSpec((tm, LANES), lambda i, g: (i, g)),
        out_shape=jax.ShapeDtypeStruct((N_ROWS, D_B), BF16),
        compiler_params=_cparams(("arbitrary", "arbitrary")),
        name="chunk_gmlp",
    )(p, p, ln_g, ln_b, w_sp, b_sp)


def _outproj_kernel(x_ref, zc_ref, zl_ref, v_ref, bsum_ref, gate_ref, ob_ref, mod_ref, w_ref, o_ref, a_ref,
                    *, ctx_tiles):
    i = pl.program_id(0)

    @pl.when(pl.program_id(1) == 0)
    def _():
        def fill(z_ref):
            oa = (z_ref[...].T + _tile_heads(bsum_ref[...]) * v_ref[...]) * gate_ref[...]
            a_ref[:, :D_A] = oa.astype(BF16)

        pl.when(i < ctx_tiles)(lambda: fill(zc_ref))
        pl.when(i >= ctx_tiles)(lambda: fill(zl_ref))
        a_ref[:, D_A:] = ob_ref[...]

    tn = o_ref.shape[1]
    j = pl.program_id(1)
    g1 = mod_ref[0, 2:3, pl.ds(pl.multiple_of(j * tn, tn), tn)]
    o_ref[...] = x_ref[...] + g1 * jnp.dot(a_ref[...], w_ref[0], preferred_element_type=F32)


def _outproj(x, z_ctx, z_lat, v, bsum, gate, ob, mod_l, w, l):
    tm, tn = 512, 1024
    per_group = GROUP_ROWS // tm
    ctx_tiles = CTX_ROWS // tm
    n_lat_tiles = (N_ROWS - CTX_ROWS) // tm
    half = pl.BlockSpec((tm, D_A), lambda i, j: (i, 0))
    return pl.pallas_call(
        functools.partial(_outproj_kernel, ctx_tiles=ctx_tiles),
        grid=(N_ROWS // tm, D_MODEL // tn),
        in_specs=[
            pl.BlockSpec((tm, tn), lambda i, j: (i, j)),
            pl.BlockSpec((D_A, tm), lambda i, j: (0, jnp.minimum(i, ctx_tiles - 1))),
            pl.BlockSpec((D_A, tm), lambda i, j: (0, jnp.clip(i - ctx_tiles, 0, n_lat_tiles - 1))),
            half,
            pl.BlockSpec((tm, LANES), lambda i, j: (i, 0)),
            half, half,
            pl.BlockSpec((1, 6, D_MODEL), lambda i, j: (i // per_group, 0, 0)),
            pl.BlockSpec((1, D_MODEL, tn), lambda i, j: (l, 0, j)),
        ],
        out_specs=pl.BlockSpec((tm, tn), lambda i, j: (i, j)),
        out_shape=jax.ShapeDtypeStruct((N_ROWS, D_MODEL), F32),
        scratch_shapes=[pltpu.VMEM((tm, D_MODEL), BF16)],
        compiler_params=_cparams(("arbitrary", "arbitrary")),
        name="outproj",
    )(x, z_ctx, z_lat, v, bsum, gate, ob, mod_l, w)


def _ffn_kernel(x_ref, mod_ref, g_ref, wg_ref, wu_ref, wd_ref, o_ref, h_ref, acc_ref):
    f = pl.program_id(1)

    @pl.when(f == 0)
    def _():
        m = mod_ref[0]
        h_ref[...] = _modnorm(x_ref[...], g_ref[...], m[4:5], m[3:4]).astype(BF16)
        acc_ref[...] = jnp.zeros_like(acc_ref)

    h = h_ref[...]
    gate = jnp.dot(h, wg_ref[0], preferred_element_type=F32)
    up = jnp.dot(h, wu_ref[0], preferred_element_type=F32)
    act = (_silu(gate) * up).astype(BF16)
    acc_ref[...] += jnp.dot(act, wd_ref[0], preferred_element_type=F32)

    @pl.when(f == pl.num_programs(1) - 1)
    def _():
        o_ref[...] = x_ref[...] + mod_ref[0, 5:6] * acc_ref[...]


def _ffn(x, mod_l, g, wg, wu, wd, l):
    tm, tf = 512, 512
    per_group = GROUP_ROWS // tm
    return pl.pallas_call(
        _ffn_kernel,
        grid=(N_ROWS // tm, D_FF // tf),
        in_specs=[
            pl.BlockSpec((tm, D_MODEL), lambda i, f: (i, 0)),
            pl.BlockSpec((1, 6, D_MODEL), lambda i, f: (i // per_group, 0, 0)),
            pl.BlockSpec((1, D_MODEL), lambda i, f: (0, 0)),
            pl.BlockSpec((1, D_MODEL, tf), lambda i, f: (l, 0, f)),
            pl.BlockSpec((1, D_MODEL, tf), lambda i, f: (l, 0, f)),
            pl.BlockSpec((1, tf, D_MODEL), lambda i, f: (l, f, 0)),
        ],
        out_specs=pl.BlockSpec((tm, D_MODEL), lambda i, f: (i, 0)),
        out_shape=jax.ShapeDtypeStruct((N_ROWS, D_MODEL), F32),
        scratch_shapes=[pltpu.VMEM((tm, D_MODEL), BF16), pltpu.VMEM((tm, D_MODEL), F32)],
        compiler_params=_cparams(("arbitrary", "arbitrary")),
        name="ffn",
    )(x, mod_l, g, wg, wu, wd)


def _final_norm_kernel(x_ref, g_ref, o_ref):
    x = x_ref[...]
    o_ref[...] = x * lax.rsqrt(jnp.mean(x * x, axis=-1, keepdims=True) + RMS_EPS) * g_ref[...]


def _final_norm(x, g, row0, rows):
    tm = 512
    return pl.pallas_call(
        _final_norm_kernel,
        grid=(rows // tm,),
        in_specs=[pl.BlockSpec((tm, D_MODEL), lambda i: (row0 // tm + i, 0)),
                  pl.BlockSpec((1, D_MODEL), lambda i: (0, 0))],
        out_specs=pl.BlockSpec((tm, D_MODEL), lambda i: (i, 0)),
        out_shape=jax.ShapeDtypeStruct((rows, D_MODEL), F32),
        compiler_params=_cparams(("arbitrary",)),
        name="final_norm",
    )(x, g)


def _head_param_tile(p):
    t = p.T.reshape(2, HALF, 1, N_HEADS)
    t = jnp.broadcast_to(t, (2, HALF, SEQ_PER_GROUP, N_HEADS))
    return t.transpose(1, 0, 2, 3).reshape(V_TILES, SUBLANES, LANES)


def _state_to_chain(s):
    g = s.shape[0]
    s = s.reshape(g, SEQ_PER_GROUP, N_HEADS, 2, HALF, HEAD)
    return s.transpose(0, 5, 4, 3, 1, 2).reshape(g, HEAD, V_TILES, SUBLANES, LANES)


def _state_from_chain(s):
    g = s.shape[0]
    s = s.reshape(g, HEAD, HALF, 2, SEQ_PER_GROUP, N_HEADS)
    return s.transpose(0, 4, 5, 3, 2, 1).reshape(g * SEQ_PER_GROUP, N_HEADS, HEAD, HEAD)


def _rwkv_stream(t_arrays, stream, s0_fwd, s0_bwd, gnw_t, gnb_t, n_rows):
    nkk_t, dec0_t, dec1_t, b0_t, b1_t, kd0_t, kd1_t, r_t, v_t = t_arrays
    nkk_c, dec0_c, dec1_c, b0_c, b1_c, kd0_c, kd1_c, r_c = _to_chain(
        (nkk_t, dec0_t, dec1_t, b0_t, b1_t, kd0_t, kd1_t, r_t), stream)
    vv_c = _value_to_chain(v_t, stream)
    z0, sf0 = _scan(nkk_c, dec0_c, b0_c, kd0_c, r_c, vv_c, s0_fwd, gnw_t, gnb_t, reverse=False)
    z1, sf1 = _scan(nkk_c, dec1_c, b1_c, kd1_c, r_c, vv_c, s0_bwd, gnw_t, gnb_t, reverse=True)
    return _from_chain(z0, z1, stream, n_rows), sf0, sf1


def _index_major(c, axis=-1):
    axis = axis % c.ndim
    split = c.reshape(c.shape[:axis] + (N_HEADS, HEAD) + c.shape[axis + 1:])
    return jnp.swapaxes(split, axis, axis + 1).reshape(c.shape)


def kernel(x_prompt, x_sample, state_rwkv, c, c_ctx, w_mod, b_mod, norm1_g, w_in, mu_shift, w0, w2, a0, a2,
           g2, k_k, k_a, r_k, gn_w, gn_b, gmlp_ln_g, gmlp_ln_b, w_spatial, b_spatial, w_out, norm2_g,
           w_ffn_gate, w_ffn_up, w_ffn_down, final_norm_g):
    x = jnp.concatenate([x_prompt.reshape(CTX_ROWS, D_MODEL), x_sample.reshape(N_LAT * LAT_LEN, D_MODEL)], axis=0)
    c_all = jnp.concatenate([c_ctx[None], c, jnp.zeros((SUBLANES - 1 - N_LAT, D_MODEL), F32)], axis=0)
    mod = _modulation(c_all, w_mod, b_mod).reshape(DEPTH, SUBLANES, 6, D_MODEL)

    def pad_cols(cols):
        c0 = 3 * D_A
        r, k, v = (_index_major(cols[..., j * D_A:(j + 1) * D_A]) for j in range(3))
        wd = cols[..., c0:c0 + DECAY_LORA]
        ad = cols[..., c0 + DECAY_LORA:c0 + DECAY_LORA + A_LORA]
        gd = cols[..., c0 + DECAY_LORA + A_LORA:c0 + DECAY_LORA + A_LORA + GATE_LORA]
        ug = cols[..., c0 + DECAY_LORA + A_LORA + GATE_LORA:]
        z = lambda n: jnp.zeros(cols.shape[:-1] + (n,), cols.dtype)
        return jnp.concatenate([r, k, v, wd, z(LANES - DECAY_LORA), ad, z(LANES - A_LORA),
                                gd, z(2 * LANES - GATE_LORA), ug], axis=-1)

    def pad_rows(w, rows):
        return jnp.pad(w, [(0, 0)] * (w.ndim - 2) + [(0, rows - w.shape[-2]), (0, 0)])

    w_in_p = pad_cols(w_in).astype(BF16)
    mu_p = pad_cols(mu_shift)[..., :COL_U]
    w0_p, a0_p = _index_major(w0), _index_major(a0)
    w2_p = pad_rows(_index_major(w2), LANES).astype(BF16)
    a2_p = pad_rows(_index_major(a2), LANES).astype(BF16)
    g2_p = pad_rows(_index_major(g2), 2 * LANES).astype(BF16)
    kk_p, ka_p = _index_major(k_k)[:, None], _index_major(k_a)[:, None]
    rk_p = _index_major(r_k.reshape(DEPTH, 1, D_A))
    w_out_p = jnp.concatenate([_index_major(w_out[:, :D_A], axis=1), w_out[:, D_A:]], axis=1).astype(BF16)
    wg_p, wu_p, wd_p = w_ffn_gate.astype(BF16), w_ffn_up.astype(BF16), w_ffn_down.astype(BF16)
    w_sp_p = w_spatial.astype(BF16)
    b_sp_p = jnp.broadcast_to(b_spatial[..., None], (DEPTH, N_GROUPS_B, CHUNK, LANES))

    ctx_groups = N_CTX_SEQ // SEQ_PER_GROUP
    s0_ctx = jnp.zeros((ctx_groups, HEAD, V_TILES, SUBLANES, LANES), F32)
    ctx_states = []
    for l in range(DEPTH):
        mod_l = mod[l]
        p = _inproj(x, mod_l, norm1_g[l][None], w_in_p, l)

        outs = _prep(p, mu_p[l], w0_p[l], a0_p[l], w2_p[l], a2_p[l], g2_p[l], kk_p[l], ka_p[l], rk_p[l],
                     n_rows=N_ROWS, ctx_rows=CTX_ROWS, image_rows=LAT_LEN)
        t_arrays, (gate, v_nat, bsum) = outs[:9], outs[9:]
        gnw_t, gnb_t = _head_param_tile(gn_w[l]), _head_param_tile(gn_b[l])

        z_ctx, sf0, sf1 = _rwkv_stream(t_arrays, CTX_STREAM, s0_ctx, s0_ctx, gnw_t, gnb_t, N_ROWS)
        s_lat = state_rwkv[:, l]
        z_lat, _, _ = _rwkv_stream(t_arrays, LAT_STREAM, _state_to_chain(s_lat[None, :, 0]),
                                   _state_to_chain(s_lat[None, :, 1]), gnw_t, gnb_t, N_ROWS)
        ctx_states.append(jnp.stack([_state_from_chain(sf0), _state_from_chain(sf1)], axis=1))

        ob = _gmlp(p, gmlp_ln_g[l], gmlp_ln_b[l], w_sp_p[l], b_sp_p[l])
        x = _outproj(x, z_ctx, z_lat, v_nat, bsum, gate, ob, mod_l, w_out_p, l)
        x = _ffn(x, mod_l, norm2_g[l][None], wg_p, wu_p, wd_p, l)

    y_prompt = _final_norm(x, final_norm_g[None], 0, CTX_ROWS).reshape(N_CTX_SEQ, CTX_LEN, D_MODEL)
    y_sample = _final_norm(x, final_norm_g[None], CTX_ROWS, N_ROWS - CTX_ROWS).reshape(N_LAT, LAT_LEN, D_MODEL)
    new_state = jnp.stack(ctx_states, axis=1)
    return (y_prompt, y_sample, new_state)
```

```python
import functools
import math

import jax
import jax.numpy as jnp
from jax import lax
from jax.experimental import pallas as pl
from jax.experimental.pallas import tpu as pltpu

F32 = jnp.float32
BF16 = jnp.bfloat16

D_MODEL = 2048
DEPTH = 4
N_CTX_SEQ = 16
CTX_LEN = 256
N_LAT = 4
LAT_LEN = 4096
GRID_W = 64
D_A = 1024
HEAD = 64
N_HEADS = 16
D_B = 1024
N_GROUPS_B = 8
CHUNK = 128
DECAY_LORA = 64
A_LORA = 64
GATE_LORA = 160
D_FF = 5632
RMS_EPS = 1e-6
GN_EPS = HEAD * 1e-5
LN_EPS = 1e-5

GROUP_ROWS = 4096
CTX_ROWS = N_CTX_SEQ * CTX_LEN
N_ROWS = CTX_ROWS + N_LAT * LAT_LEN
LANES = 128
SUBLANES = 8
SEQ_PER_GROUP = 4
HALF = HEAD // 2

COL_R, COL_K, COL_V = 0, D_A, 2 * D_A
COL_LORA = 3 * D_A
LORA_W = 512
COL_U = COL_LORA + LORA_W
COL_G = COL_U + D_B
P_PAD = COL_G + D_B

VMEM_LIMIT = 48 * 1024 * 1024

CTX_STREAM = (0, N_CTX_SEQ, CTX_LEN)
LAT_STREAM = (CTX_ROWS, N_LAT, LAT_LEN)


def _cparams(sem):
    return pltpu.CompilerParams(dimension_semantics=sem, vmem_limit_bytes=VMEM_LIMIT)


def _silu(x):
    return x * jax.nn.sigmoid(x)


def _gelu_tanh(x):
    return 0.5 * x * (1.0 + jnp.tanh(math.sqrt(2.0 / math.pi) * (x + 0.044715 * (x * x * x))))


def _modnorm(x, g, scale, shift):
    y = x * lax.rsqrt(jnp.mean(x * x, axis=-1, keepdims=True) + RMS_EPS)
    return (y * g) * (1.0 + scale) + shift


def _mod_kernel(c_ref, w_ref, b_ref, o_ref):
    a = _silu(c_ref[...]).astype(BF16)
    o_ref[0] = jnp.dot(a, w_ref[0].astype(BF16), preferred_element_type=F32) + b_ref[0]


def _modulation(c_all, w_mod, b_mod):
    tn = 1024
    n6 = 6 * D_MODEL
    return pl.pallas_call(
        _mod_kernel,
        grid=(DEPTH, n6 // tn),
        in_specs=[
            pl.BlockSpec((SUBLANES, D_MODEL), lambda l, j: (0, 0)),
            pl.BlockSpec((1, D_MODEL, tn), lambda l, j: (l, 0, j)),
            pl.BlockSpec((1, 1, tn), lambda l, j: (l, 0, j)),
        ],
        out_specs=pl.BlockSpec((1, SUBLANES, tn), lambda l, j: (l, 0, j)),
        out_shape=jax.ShapeDtypeStruct((DEPTH, SUBLANES, n6), F32),
        compiler_params=_cparams(("arbitrary", "arbitrary")),
        name="modulation",
    )(c_all, w_mod, b_mod.reshape(DEPTH, 1, n6))


def _inproj_kernel(x_ref, mod_ref, g_ref, w_ref, o_ref, h_ref):
    @pl.when(pl.program_id(1) == 0)
    def _():
        m = mod_ref[0]
        h_ref[...] = _modnorm(x_ref[...], g_ref[...], m[1:2], m[0:1]).astype(BF16)

    o_ref[...] = jnp.dot(h_ref[...], w_ref[0], preferred_element_type=F32)


def _inproj(x, mod_l, g, w, l):
    tm, tn = 1024, 512
    per_group = GROUP_ROWS // tm
    return pl.pallas_call(
        _inproj_kernel,
        grid=(N_ROWS // tm, P_PAD // tn),
        in_specs=[
            pl.BlockSpec((tm, D_MODEL), lambda i, j: (i, 0)),
            pl.BlockSpec((1, 6, D_MODEL), lambda i, j: (i // per_group, 0, 0)),
            pl.BlockSpec((1, D_MODEL), lambda i, j: (0, 0)),
            pl.BlockSpec((1, D_MODEL, tn), lambda i, j: (l, 0, j)),
        ],
        out_specs=pl.BlockSpec((tm, tn), lambda i, j: (i, j)),
        out_shape=jax.ShapeDtypeStruct((N_ROWS, P_PAD), F32),
        scratch_shapes=[pltpu.VMEM((tm, D_MODEL), BF16)],
        compiler_params=_cparams(("arbitrary", "arbitrary")),
        name="inproj",
    )(x, mod_l, g, w)


PREP_TM = 256
HALO = GRID_W


def _shift(x, top, bot, mu, is_ctx, has_top, has_bot):
    tm = x.shape[0]
    assert tm % CTX_LEN == 0 and tm % GRID_W == 0
    period = jnp.where(is_ctx, CTX_LEN, GRID_W)
    pos = lax.broadcasted_iota(jnp.int32, (tm, 1), 0) & (period - 1)
    prev = jnp.where(pos == 0, 0.0, pltpu.roll(x, 1, axis=0))
    nxt = jnp.where(pos == period - 1, 0.0, pltpu.roll(x, tm - 1, axis=0))
    up = jnp.concatenate([jnp.where(has_top, top, 0.0), x[: tm - HALO]], axis=0)
    down = jnp.concatenate([x[HALO:], jnp.where(has_bot, bot, 0.0)], axis=0)
    mu_up = jnp.where(is_ctx, 0.0, mu[2:3])
    mu_down = jnp.where(is_ctx, 0.0, mu[3:4])
    centre = 1.0 - mu[0:1] - mu[1:2] - mu_up - mu_down
    return x * centre + mu[0:1] * prev + mu[1:2] * nxt + mu_up * up + mu_down * down


def _head_sum(x):
    s = x[:, 0:LANES]
    for j in range(1, D_A // LANES):
        s = s + x[:, j * LANES:(j + 1) * LANES]
    for sh in (N_HEADS, 2 * N_HEADS, 4 * N_HEADS):
        s = s + pltpu.roll(s, sh, axis=1)
    return s


def _tile_heads(s):
    return jnp.concatenate([s] * (D_A // LANES), axis=1)


def _prep_kernel(r_ref, rt_ref, rb_ref, k_ref, kt_ref, kb_ref, v_ref, vt_ref, vb_ref,
                 l_ref, lt_ref, lb_ref, mu_ref, w0_ref, a0_ref, w2_ref, a2_ref, g2_ref,
                 kk_ref, ka_ref, rk_ref,
                 nkk_o, dec0_o, dec1_o, b0_o, b1_o, kd0_o, kd1_o, r_o, v_o, gate_o, vnat_o, bsum_o,
                 *, ctx_tiles, tiles_per_image):
    i = pl.program_id(0)
    is_ctx = i < ctx_tiles
    img_tile = (i - ctx_tiles) % tiles_per_image
    has_top = img_tile != 0
    has_bot = img_tile != tiles_per_image - 1
    sh = functools.partial(_shift, is_ctx=is_ctx, has_top=has_top, has_bot=has_bot)

    r = sh(r_ref[...], rt_ref[...], rb_ref[...], mu_ref[:, COL_R:COL_R + D_A])
    k = sh(k_ref[...], kt_ref[...], kb_ref[...], mu_ref[:, COL_K:COL_K + D_A])
    v = sh(v_ref[...], vt_ref[...], vb_ref[...], mu_ref[:, COL_V:COL_V + D_A])
    lo = sh(l_ref[...], lt_ref[...], lb_ref[...], mu_ref[:, COL_LORA:COL_LORA + LORA_W])
    wt = jnp.tanh(lo[:, 0:LANES]).astype(BF16)
    ad = lo[:, LANES:2 * LANES].astype(BF16)
    gs = jax.nn.sigmoid(lo[:, 2 * LANES:]).astype(BF16)

    kk = k * kk_ref[...]
    inv = lax.rsqrt(jnp.maximum(_head_sum(kk * kk), 1e-24))
    kk = kk * _tile_heads(inv)
    r_o[...] = r.T
    v_o[...] = v.T
    vnat_o[...] = v
    nkk_o[...] = (-kk).T
    gate_o[...] = jnp.dot(gs, g2_ref[...], preferred_element_type=F32)
    ka = ka_ref[...]
    rrk = r * rk_ref[...]
    bsum = None
    for d, (dec_o, b_o, kd_o) in enumerate(((dec0_o, b0_o, kd0_o), (dec1_o, b1_o, kd1_o))):
        wl = w0_ref[d:d + 1] + jnp.dot(wt, w2_ref[d], preferred_element_type=F32)
        dec_o[...] = jnp.exp(-(math.exp(-0.5) * jax.nn.sigmoid(wl))).T
        a = jax.nn.sigmoid(a0_ref[d:d + 1] + jnp.dot(ad, a2_ref[d], preferred_element_type=F32))
        kd = k * (1.0 + (a - 1.0) * ka)
        kd_o[...] = kd.T
        b_o[...] = (kk * a).T
        bonus = _head_sum(rrk * kd)
        bsum = bonus if bsum is None else bsum + bonus
    bsum_o[...] = bsum


def _prep(p, mu, w0, a0, w2, a2, g2, k_k, k_a, r_k, *, n_rows, ctx_rows, image_rows):
    tm = PREP_TM
    hb = tm // HALO
    n_halo = n_rows // HALO

    def main(col0, width):
        return pl.BlockSpec((tm, width), lambda i: (i, col0 // width))

    def top(col0, width):
        return pl.BlockSpec((HALO, width), lambda i: (jnp.maximum(i * hb - 1, 0), col0 // width))

    def bot(col0, width):
        return pl.BlockSpec((HALO, width), lambda i: (jnp.minimum(i * hb + hb, n_halo - 1), col0 // width))

    def full(a):
        nd = a.ndim
        return pl.BlockSpec(a.shape, lambda i: (0,) * nd)

    in_specs = []
    for col0 in (COL_R, COL_K, COL_V):
        in_specs += [main(col0, D_A), top(col0, D_A), bot(col0, D_A)]
    in_specs += [main(COL_LORA, LORA_W), top(COL_LORA, LORA_W), bot(COL_LORA, LORA_W)]
    params = (mu, w0, a0, w2, a2, g2, k_k, k_a, r_k)
    in_specs += [full(a) for a in params]
    t_spec = pl.BlockSpec((D_A, tm), lambda i: (0, i))
    n_spec = pl.BlockSpec((tm, D_A), lambda i: (i, 0))
    t_shape = jax.ShapeDtypeStruct((D_A, n_rows), F32)
    n_shape = jax.ShapeDtypeStruct((n_rows, D_A), F32)
    return pl.pallas_call(
        functools.partial(_prep_kernel, ctx_tiles=ctx_rows // tm, tiles_per_image=image_rows // tm),
        grid=(n_rows // tm,),
        in_specs=in_specs,
        out_specs=[t_spec] * 9 + [n_spec, n_spec, pl.BlockSpec((tm, LANES), lambda i: (i, 0))],
        out_shape=[t_shape] * 9 + [n_shape, n_shape, jax.ShapeDtypeStruct((n_rows, LANES), F32)],
        compiler_params=_cparams(("arbitrary",)),
        name="rwkv_prep",
    )(p, p, p, p, p, p, p, p, p, p, p, p, *params)


RELAYOUT_T = 256
FROM_CHAIN_T = 128
K_PER_BLOCK = LANES // N_HEADS


def _seq_col_block(stream, g, b, tb):
    row0, _, t_len = stream
    return (row0 + (g * SEQ_PER_GROUP + b) * t_len) // RELAYOUT_T + tb


def _to_chain_kernel(*refs, n_arrays):
    ins, outs = refs[:SEQ_PER_GROUP * n_arrays], refs[SEQ_PER_GROUP * n_arrays:]
    for a in range(n_arrays):
        blks = [ins[SEQ_PER_GROUP * a + b][...] for b in range(SEQ_PER_GROUP)]
        for kk in range(K_PER_BLOCK):
            slabs = [blk[kk * N_HEADS:(kk + 1) * N_HEADS] for blk in blks]
            outs[a][0, kk] = jnp.concatenate(slabs + slabs, axis=0).T


def _to_chain(arrays, stream):
    _, n_seq, t_len = stream
    groups = n_seq // SEQ_PER_GROUP
    n = len(arrays)
    in_specs, operands = [], []
    for a in arrays:
        for b in range(SEQ_PER_GROUP):
            in_specs.append(pl.BlockSpec(
                (LANES, RELAYOUT_T),
                functools.partial(lambda b, g, tb, kb: (kb, _seq_col_block(stream, g, b, tb)), b)))
            operands.append(a)
    out_spec = pl.BlockSpec((1, K_PER_BLOCK, RELAYOUT_T, LANES), lambda g, tb, kb: (g, kb, tb, 0))
    out_shape = jax.ShapeDtypeStruct((groups, HEAD, t_len, LANES), F32)
    return pl.pallas_call(
        functools.partial(_to_chain_kernel, n_arrays=n),
        grid=(groups, t_len // RELAYOUT_T, HEAD // K_PER_BLOCK),
        in_specs=in_specs,
        out_specs=[out_spec] * n,
        out_shape=[out_shape] * n,
        compiler_params=_cparams(("arbitrary", "arbitrary", "arbitrary")),
        name="to_chain",
    )(*operands)


def _value_to_chain_kernel(v0_ref, v1_ref, v2_ref, v3_ref, o_ref):
    blks = [r[...] for r in (v0_ref, v1_ref, v2_ref, v3_ref)]
    for vl in range(HALF):
        slabs = [blk[(vh * HALF + vl) * N_HEADS:(vh * HALF + vl + 1) * N_HEADS]
                 for vh in range(2) for blk in blks]
        o_ref[0, pl.ds(vl, RELAYOUT_T, stride=HALF), :] = jnp.concatenate(slabs, axis=0).T


def _value_to_chain(v_t, stream):
    _, n_seq, t_len = stream
    groups = n_seq // SEQ_PER_GROUP
    in_specs = [pl.BlockSpec((D_A, RELAYOUT_T),
                             functools.partial(lambda b, g, tb: (0, _seq_col_block(stream, g, b, tb)), b))
                for b in range(SEQ_PER_GROUP)]
    return pl.pallas_call(
        _value_to_chain_kernel,
        grid=(groups, t_len // RELAYOUT_T),
        in_specs=in_specs,
        out_specs=pl.BlockSpec((1, RELAYOUT_T * HALF, LANES), lambda g, tb: (g, tb, 0)),
        out_shape=jax.ShapeDtypeStruct((groups, t_len * HALF, LANES), F32),
        compiler_params=_cparams(("arbitrary", "arbitrary")),
        name="value_to_chain",
    )(v_t, v_t, v_t, v_t)


def _from_chain_kernel(z0_ref, z1_ref, o_ref, t_ref):
    b = pl.program_id(2)

    @pl.when(b == 0)
    def _():
        for vl in range(HALF):
            rows = pl.ds(vl, FROM_CHAIN_T, stride=HALF)
            t_ref[vl] = (z0_ref[0, rows, :] + z1_ref[0, rows, :]).T

    for vl in range(HALF):
        for vh in range(2):
            src = pl.ds(pl.multiple_of(vh * (LANES // 2) + b * N_HEADS, N_HEADS), N_HEADS)
            o_ref[(vh * HALF + vl) * N_HEADS:(vh * HALF + vl + 1) * N_HEADS, :] = t_ref[vl, src, :]


def _from_chain(z0, z1, stream, n_rows):
    _, n_seq, t_len = stream
    groups = n_seq // SEQ_PER_GROUP
    nt = t_len // FROM_CHAIN_T
    z_spec = pl.BlockSpec((1, FROM_CHAIN_T * HALF, LANES), lambda g, tb, b: (g, tb, 0))
    return pl.pallas_call(
        _from_chain_kernel,
        grid=(groups, nt, SEQ_PER_GROUP),
        in_specs=[z_spec, z_spec],
        out_specs=pl.BlockSpec((D_A, FROM_CHAIN_T), lambda g, tb, b: (0, (g * SEQ_PER_GROUP + b) * nt + tb)),
        out_shape=jax.ShapeDtypeStruct((D_A, n_seq * t_len), F32),
        scratch_shapes=[pltpu.VMEM((HALF, LANES, FROM_CHAIN_T), F32)],
        compiler_params=_cparams(("arbitrary", "arbitrary", "arbitrary")),
        name="from_chain",
    )(z0, z1)


SCAN_TT = 64
V_TILES = HALF // SUBLANES
ACC_PARTS = 2


def _scan_kernel(a_ref, w_ref, b_ref, kd_ref, r_ref, vv_ref, s0_ref, gnw_ref, gnb_ref,
                 z_ref, sfin_ref, s_ref, *, reverse):
    @pl.when(pl.program_id(1) == 0)
    def _():
        s_ref[...] = s0_ref[0]

    gnw = gnw_ref[...]
    gnb = gnb_ref[...]

    def block_row(i):
        return SCAN_TT - 1 - i if reverse else i

    def row(ref, k, ii):
        return ref[0, k, pl.ds(ii, 1), :]

    def value_rows(ii):
        start = ii * HALF
        return pl.ds(start if isinstance(start, int) else pl.multiple_of(start, HALF), HALF)

    def accumulate(parts, k, term):
        j = k % ACC_PARTS
        parts[j] = term if parts[j] is None else parts[j] + term

    def total(parts):
        out = parts[0]
        for p in parts[1:]:
            out = out + p
        return out

    def group_norm_store(y, ii):
        def head_mean(x):
            part = jnp.sum(jnp.sum(x, axis=0), axis=0, keepdims=True)
            return (part + pltpu.roll(part, LANES // 2, axis=1)) * (1.0 / HEAD)

        yc = y - head_mean(y)
        z = yc * lax.rsqrt(head_mean(yc * yc) + GN_EPS) * gnw + gnb
        z_ref[0, value_rows(ii), :] = z.reshape(HALF, LANES)

    first = block_row(0)
    parts = [None] * ACC_PARTS
    for k in range(HEAD):
        accumulate(parts, k, s_ref[k] * row(a_ref, k, first))

    def step(i, carry):
        sa, y_prev = carry
        ii = block_row(i)
        ii_next = block_row(jnp.minimum(i + 1, SCAN_TT - 1))
        group_norm_store(y_prev, block_row(jnp.maximum(i - 1, 0)))
        vv = vv_ref[0, value_rows(ii), :].reshape(V_TILES, SUBLANES, LANES)
        y_parts = [None] * ACC_PARTS
        sa_parts = [None] * ACC_PARTS
        for k in range(HEAD):
            s = s_ref[k] * row(w_ref, k, ii) + sa * row(b_ref, k, ii) + vv * row(kd_ref, k, ii)
            s_ref[k] = s
            accumulate(y_parts, k, s * row(r_ref, k, ii))
            accumulate(sa_parts, k, s * row(a_ref, k, ii_next))
        return total(sa_parts), total(y_parts)

    zero = jnp.zeros((V_TILES, SUBLANES, LANES), F32)
    _, y_last = lax.fori_loop(0, SCAN_TT, step, (total(parts), zero), unroll=4)
    group_norm_store(y_last, block_row(SCAN_TT - 1))

    @pl.when(pl.program_id(1) == pl.num_programs(1) - 1)
    def _():
        sfin_ref[0] = s_ref[...]


def _scan(a, w, b, kd, r, vv, s0, gnw, gnb, *, reverse):
    g, t = a.shape[0], a.shape[2]
    tt = SCAN_TT
    nt = t // tt
    tblk = (lambda ti: nt - 1 - ti) if reverse else (lambda ti: ti)
    kspec = pl.BlockSpec((1, HEAD, tt, LANES), lambda gi, ti: (gi, 0, tblk(ti), 0))
    vspec = pl.BlockSpec((1, tt * HALF, LANES), lambda gi, ti: (gi, tblk(ti), 0))
    sspec = pl.BlockSpec((1, HEAD, V_TILES, SUBLANES, LANES), lambda gi, ti: (gi, 0, 0, 0, 0))
    pspec = pl.BlockSpec((V_TILES, SUBLANES, LANES), lambda gi, ti: (0, 0, 0))
    return pl.pallas_call(
        functools.partial(_scan_kernel, reverse=reverse),
        grid=(g, nt),
        in_specs=[kspec] * 5 + [vspec, sspec, pspec, pspec],
        out_specs=[vspec, sspec],
        out_shape=[jax.ShapeDtypeStruct((g, t * HALF, LANES), F32),
                   jax.ShapeDtypeStruct((g, HEAD, V_TILES, SUBLANES, LANES), F32)],
        scratch_shapes=[pltpu.VMEM((HEAD, V_TILES, SUBLANES, LANES), F32)],
        compiler_params=_cparams(("arbitrary", "arbitrary")),
        name="delta_scan_bwd" if reverse else "delta_scan_fwd",
    )(a, w, b, kd, r, vv, s0, gnw, gnb)


def _gmlp_kernel(u_ref, v_ref, lng_ref, lnb_ref, ws_ref, bs_ref, o_ref):
    g = pl.program_id(1)
    v = _gelu_tanh(v_ref[...])
    mean = jnp.mean(v, axis=-1, keepdims=True)
    vc = v - mean
    var = jnp.mean(vc * vc, axis=-1, keepdims=True)
    vn = (vc * lax.rsqrt(var + LN_EPS) * lng_ref[pl.ds(g, 1), :] + lnb_ref[pl.ds(g, 1), :]).astype(BF16)
    u = _gelu_tanh(u_ref[...])
    ws = ws_ref[0]
    bs = bs_ref[0]
    for c in range(u.shape[0] // CHUNK):
        rows = slice(c * CHUNK, (c + 1) * CHUNK)
        s = jnp.dot(ws, vn[rows], preferred_element_type=F32) + bs
        o_ref[rows, :] = (u[rows] * s).astype(o_ref.dtype)


def _gmlp(p, ln_g, ln_b, w_sp, b_sp):
    tm = 2048
    ub, gb = COL_U // LANES, COL_G // LANES
    return pl.pallas_call(
        _gmlp_kernel,
        grid=(N_ROWS // tm, N_GROUPS_B),
        in_specs=[
            pl.BlockSpec((tm, LANES), lambda i, g: (i, ub + g)),
            pl.BlockSpec((tm, LANES), lambda i, g: (i, gb + g)),
            pl.BlockSpec((N_GROUPS_B, LANES), lambda i, g: (0, 0)),
            pl.BlockSpec((N_GROUPS_B, LANES), lambda i, g: (0, 0)),
            pl.BlockSpec((1, CHUNK, CHUNK), lambda i, g: (g, 0, 0)),
            pl.BlockSpec((1, CHUNK, LANES), lambda i, g: (g, 0, 0)),
        ],
        out_specs=pl.BlockSpec((tm, LANES), lambda i, g: (i, g)),
        out_shape=jax.ShapeDtypeStruct((N_ROWS, D_B), BF16),
        compiler_params=_cparams(("arbitrary", "arbitrary")),
        name="chunk_gmlp",
    )(p, p, ln_g, ln_b, w_sp, b_sp)


def _outproj_kernel(x_ref, zc_ref, zl_ref, v_ref, bsum_ref, gate_ref, ob_ref, mod_ref, w_ref, o_ref, a_ref,
                    *, ctx_tiles):
    i = pl.program_id(0)

    @pl.when(pl.program_id(1) == 0)
    def _():
        def fill(z_ref):
            oa = (z_ref[...].T + _tile_heads(bsum_ref[...]) * v_ref[...]) * gate_ref[...]
            a_ref[:, :D_A] = oa.astype(BF16)

        pl.when(i < ctx_tiles)(lambda: fill(zc_ref))
        pl.when(i >= ctx_tiles)(lambda: fill(zl_ref))
        a_ref[:, D_A:] = ob_ref[...]

    tn = o_ref.shape[1]
    j = pl.program_id(1)
    g1 = mod_ref[0, 2:3, pl.ds(pl.multiple_of(j * tn, tn), tn)]
    o_ref[...] = x_ref[...] + g1 * jnp.dot(a_ref[...], w_ref[0], preferred_element_type=F32)


def _outproj(x, z_ctx, z_lat, v, bsum, gate, ob, mod_l, w, l):
    tm, tn = 512, 1024
    per_group = GROUP_ROWS // tm
    ctx_tiles = CTX_ROWS // tm
    n_lat_tiles = (N_ROWS - CTX_ROWS) // tm
    half = pl.BlockSpec((tm, D_A), lambda i, j: (i, 0))
    return pl.pallas_call(
        functools.partial(_outproj_kernel, ctx_tiles=ctx_tiles),
        grid=(N_ROWS // tm, D_MODEL // tn),
        in_specs=[
            pl.BlockSpec((tm, tn), lambda i, j: (i, j)),
            pl.BlockSpec((D_A, tm), lambda i, j: (0, jnp.minimum(i, ctx_tiles - 1))),
            pl.BlockSpec((D_A, tm), lambda i, j: (0, jnp.clip(i - ctx_tiles, 0, n_lat_tiles - 1))),
            half,
            pl.BlockSpec((tm, LANES), lambda i, j: (i, 0)),
            half, half,
            pl.BlockSpec((1, 6, D_MODEL), lambda i, j: (i // per_group, 0, 0)),
            pl.BlockSpec((1, D_MODEL, tn), lambda i, j: (l, 0, j)),
        ],
        out_specs=pl.BlockSpec((tm, tn), lambda i, j: (i, j)),
        out_shape=jax.ShapeDtypeStruct((N_ROWS, D_MODEL), F32),
        scratch_shapes=[pltpu.VMEM((tm, D_MODEL), BF16)],
        compiler_params=_cparams(("arbitrary", "arbitrary")),
        name="outproj",
    )(x, z_ctx, z_lat, v, bsum, gate, ob, mod_l, w)


def _ffn_kernel(x_ref, mod_ref, g_ref, wg_ref, wu_ref, wd_ref, o_ref, h_ref, acc_ref):
    f = pl.program_id(1)

    @pl.when(f == 0)
    def _():
        m = mod_ref[0]
        h_ref[...] = _modnorm(x_ref[...], g_ref[...], m[4:5], m[3:4]).astype(BF16)
        acc_ref[...] = jnp.zeros_like(acc_ref)

    h = h_ref[...]
    gate = jnp.dot(h, wg_ref[0], preferred_element_type=F32)
    up = jnp.dot(h, wu_ref[0], preferred_element_type=F32)
    act = (_silu(gate) * up).astype(BF16)
    acc_ref[...] += jnp.dot(act, wd_ref[0], preferred_element_type=F32)

    @pl.when(f == pl.num_programs(1) - 1)
    def _():
        o_ref[...] = x_ref[...] + mod_ref[0, 5:6] * acc_ref[...]


def _ffn(x, mod_l, g, wg, wu, wd, l):
    tm, tf = 512, 512
    per_group = GROUP_ROWS // tm
    return pl.pallas_call(
        _ffn_kernel,
        grid=(N_ROWS // tm, D_FF // tf),
        in_specs=[
            pl.BlockSpec((tm, D_MODEL), lambda i, f: (i, 0)),
            pl.BlockSpec((1, 6, D_MODEL), lambda i, f: (i // per_group, 0, 0)),
            pl.BlockSpec((1, D_MODEL), lambda i, f: (0, 0)),
            pl.BlockSpec((1, D_MODEL, tf), lambda i, f: (l, 0, f)),
            pl.BlockSpec((1, D_MODEL, tf), lambda i, f: (l, 0, f)),
            pl.BlockSpec((1, tf, D_MODEL), lambda i, f: (l, f, 0)),
        ],
        out_specs=pl.BlockSpec((tm, D_MODEL), lambda i, f: (i, 0)),
        out_shape=jax.ShapeDtypeStruct((N_ROWS, D_MODEL), F32),
        scratch_shapes=[pltpu.VMEM((tm, D_MODEL), BF16), pltpu.VMEM((tm, D_MODEL), F32)],
        compiler_params=_cparams(("arbitrary", "arbitrary")),
        name="ffn",
    )(x, mod_l, g, wg, wu, wd)


def _final_norm_kernel(x_ref, g_ref, o_ref):
    x = x_ref[...]
    o_ref[...] = x * lax.rsqrt(jnp.mean(x * x, axis=-1, keepdims=True) + RMS_EPS) * g_ref[...]


def _final_norm(x, g, row0, rows):
    tm = 512
    return pl.pallas_call(
        _final_norm_kernel,
        grid=(rows // tm,),
        in_specs=[pl.BlockSpec((tm, D_MODEL), lambda i: (row0 // tm + i, 0)),
                  pl.BlockSpec((1, D_MODEL), lambda i: (0, 0))],
        out_specs=pl.BlockSpec((tm, D_MODEL), lambda i: (i, 0)),
        out_shape=jax.ShapeDtypeStruct((rows, D_MODEL), F32),
        compiler_params=_cparams(("arbitrary",)),
        name="final_norm",
    )(x, g)


def _head_param_tile(p):
    t = p.T.reshape(2, HALF, 1, N_HEADS)
    t = jnp.broadcast_to(t, (2, HALF, SEQ_PER_GROUP, N_HEADS))
    return t.transpose(1, 0, 2, 3).reshape(V_TILES, SUBLANES, LANES)


def _state_to_chain(s):
    g = s.shape[0]
    s = s.reshape(g, SEQ_PER_GROUP, N_HEADS, 2, HALF, HEAD)
    return s.transpose(0, 5, 4, 3, 1, 2).reshape(g, HEAD, V_TILES, SUBLANES, LANES)


def _state_from_chain(s):
    g = s.shape[0]
    s = s.reshape(g, HEAD, HALF, 2, SEQ_PER_GROUP, N_HEADS)
    return s.transpose(0, 4, 5, 3, 2, 1).reshape(g * SEQ_PER_GROUP, N_HEADS, HEAD, HEAD)


def _rwkv_stream(t_arrays, stream, s0_fwd, s0_bwd, gnw_t, gnb_t, n_rows):
    nkk_t, dec0_t, dec1_t, b0_t, b1_t, kd0_t, kd1_t, r_t, v_t = t_arrays
    nkk_c, dec0_c, dec1_c, b0_c, b1_c, kd0_c, kd1_c, r_c = _to_chain(
        (nkk_t, dec0_t, dec1_t, b0_t, b1_t, kd0_t, kd1_t, r_t), stream)
    vv_c = _value_to_chain(v_t, stream)
    z0, sf0 = _scan(nkk_c, dec0_c, b0_c, kd0_c, r_c, vv_c, s0_fwd, gnw_t, gnb_t, reverse=False)
    z1, sf1 = _scan(nkk_c, dec1_c, b1_c, kd1_c, r_c, vv_c, s0_bwd, gnw_t, gnb_t, reverse=True)
    return _from_chain(z0, z1, stream, n_rows), sf0, sf1


def _index_major(c, axis=-1):
    axis = axis % c.ndim
    split = c.reshape(c.shape[:axis] + (N_HEADS, HEAD) + c.shape[axis + 1:])
    return jnp.swapaxes(split, axis, axis + 1).reshape(c.shape)


def kernel(x_prompt, x_sample, state_rwkv, c, c_ctx, w_mod, b_mod, norm1_g, w_in, mu_shift, w0, w2, a0, a2,
           g2, k_k, k_a, r_k, gn_w, gn_b, gmlp_ln_g, gmlp_ln_b, w_spatial, b_spatial, w_out, norm2_g,
           w_ffn_gate, w_ffn_up, w_ffn_down, final_norm_g):
    x = jnp.concatenate([x_prompt.reshape(CTX_ROWS, D_MODEL), x_sample.reshape(N_LAT * LAT_LEN, D_MODEL)], axis=0)
    c_all = jnp.concatenate([c_ctx[None], c, jnp.zeros((SUBLANES - 1 - N_LAT, D_MODEL), F32)], axis=0)
    mod = _modulation(c_all, w_mod, b_mod).reshape(DEPTH, SUBLANES, 6, D_MODEL)

    def pad_cols(cols):
        c0 = 3 * D_A
        r, k, v = (_index_major(cols[..., j * D_A:(j + 1) * D_A]) for j in range(3))
        wd = cols[..., c0:c0 + DECAY_LORA]
        ad = cols[..., c0 + DECAY_LORA:c0 + DECAY_LORA + A_LORA]
        gd = cols[..., c0 + DECAY_LORA + A_LORA:c0 + DECAY_LORA + A_LORA + GATE_LORA]
        ug = cols[..., c0 + DECAY_LORA + A_LORA + GATE_LORA:]
        z = lambda n: jnp.zeros(cols.shape[:-1] + (n,), cols.dtype)
        return jnp.concatenate([r, k, v, wd, z(LANES - DECAY_LORA), ad, z(LANES - A_LORA),
                                gd, z(2 * LANES - GATE_LORA), ug], axis=-1)

    def pad_rows(w, rows):
        return jnp.pad(w, [(0, 0)] * (w.ndim - 2) + [(0, rows - w.shape[-2]), (0, 0)])

    w_in_p = pad_cols(w_in).astype(BF16)
    mu_p = pad_cols(mu_shift)[..., :COL_U]
    w0_p, a0_p = _index_major(w0), _index_major(a0)
    w2_p = pad_rows(_index_major(w2), LANES).astype(BF16)
    a2_p = pad_rows(_index_major(a2), LANES).astype(BF16)
    g2_p = pad_rows(_index_major(g2), 2 * LANES).astype(BF16)
    kk_p, ka_p = _index_major(k_k)[:, None], _index_major(k_a)[:, None]
    rk_p = _index_major(r_k.reshape(DEPTH, 1, D_A))
    w_out_p = jnp.concatenate([_index_major(w_out[:, :D_A], axis=1), w_out[:, D_A:]], axis=1).astype(BF16)
    wg_p, wu_p, wd_p = w_ffn_gate.astype(BF16), w_ffn_up.astype(BF16), w_ffn_down.astype(BF16)
    w_sp_p = w_spatial.astype(BF16)
    b_sp_p = jnp.broadcast_to(b_spatial[..., None], (DEPTH, N_GROUPS_B, CHUNK, LANES))

    ctx_groups = N_CTX_SEQ // SEQ_PER_GROUP
    s0_ctx = jnp.zeros((ctx_groups, HEAD, V_TILES, SUBLANES, LANES), F32)
    ctx_states = []
    for l in range(DEPTH):
        mod_l = mod[l]
        p = _inproj(x, mod_l, norm1_g[l][None], w_in_p, l)

        outs = _prep(p, mu_p[l], w0_p[l], a0_p[l], w2_p[l], a2_p[l], g2_p[l], kk_p[l], ka_p[l], rk_p[l],
                     n_rows=N_ROWS, ctx_rows=CTX_ROWS, image_rows=LAT_LEN)
        t_arrays, (gate, v_nat, bsum) = outs[:9], outs[9:]
        gnw_t, gnb_t = _head_param_tile(gn_w[l]), _head_param_tile(gn_b[l])

        z_ctx, sf0, sf1 = _rwkv_stream(t_arrays, CTX_STREAM, s0_ctx, s0_ctx, gnw_t, gnb_t, N_ROWS)
        s_lat = state_rwkv[:, l]
        z_lat, _, _ = _rwkv_stream(t_arrays, LAT_STREAM, _state_to_chain(s_lat[None, :, 0]),
                                   _state_to_chain(s_lat[None, :, 1]), gnw_t, gnb_t, N_ROWS)
        ctx_states.append(jnp.stack([_state_from_chain(sf0), _state_from_chain(sf1)], axis=1))

        ob = _gmlp(p, gmlp_ln_g[l], gmlp_ln_b[l], w_sp_p[l], b_sp_p[l])
        x = _outproj(x, z_ctx, z_lat, v_nat, bsum, gate, ob, mod_l, w_out_p, l)
        x = _ffn(x, mod_l, norm2_g[l][None], wg_p, wu_p, wd_p, l)

    y_prompt = _final_norm(x, final_norm_g[None], 0, CTX_ROWS).reshape(N_CTX_SEQ, CTX_LEN, D_MODEL)
    y_sample = _final_norm(x, final_norm_g[None], CTX_ROWS, N_ROWS - CTX_ROWS).reshape(N_LAT, LAT_LEN, D_MODEL)
    new_state = jnp.stack(ctx_states, axis=1)
    return (y_prompt, y_sample, new_state)
```

```python
import functools
import math

import jax
import jax.numpy as jnp
from jax import lax
from jax.experimental import pallas as pl
from jax.experimental.pallas import tpu as pltpu

F32 = jnp.float32
BF16 = jnp.bfloat16

D_MODEL = 2048
DEPTH = 4
N_CTX_SEQ = 16
CTX_LEN = 256
N_LAT = 4
LAT_LEN = 4096
GRID_W = 64
D_A = 1024
HEAD = 64
N_HEADS = 16
D_B = 1024
N_GROUPS_B = 8
CHUNK = 128
DECAY_LORA = 64
A_LORA = 64
GATE_LORA = 160
D_FF = 5632
RMS_EPS = 1e-6
GN_EPS = HEAD * 1e-5
LN_EPS = 1e-5

GROUP_ROWS = 4096
CTX_ROWS = N_CTX_SEQ * CTX_LEN
N_ROWS = CTX_ROWS + N_LAT * LAT_LEN
LANES = 128
SUBLANES = 8
SEQ_PER_GROUP = 4
HALF = HEAD // 2

COL_R, COL_K, COL_V = 0, D_A, 2 * D_A
COL_LORA = 3 * D_A
LORA_W = 512
COL_U = COL_LORA + LORA_W
COL_G = COL_U + D_B
P_PAD = COL_G + D_B

VMEM_LIMIT = 48 * 1024 * 1024

CTX_STREAM = (0, N_CTX_SEQ, CTX_LEN)
LAT_STREAM = (CTX_ROWS, N_LAT, LAT_LEN)


def _cparams(sem):
    return pltpu.CompilerParams(dimension_semantics=sem, vmem_limit_bytes=VMEM_LIMIT)


def _silu(x):
    return x * jax.nn.sigmoid(x)


def _gelu_tanh(x):
    return 0.5 * x * (1.0 + jnp.tanh(math.sqrt(2.0 / math.pi) * (x + 0.044715 * (x * x * x))))


def _modnorm(x, g, scale, shift):
    y = x * lax.rsqrt(jnp.mean(x * x, axis=-1, keepdims=True) + RMS_EPS)
    return (y * g) * (1.0 + scale) + shift


NORM_CHUNK = 32


def _modnorm_store(x_ref, h_ref, ss_ref, g, scale, shift):
    n = x_ref.shape[0] // NORM_CHUNK

    def rows_of(c):
        return pl.ds(pl.multiple_of(c * NORM_CHUNK, NORM_CHUNK), NORM_CHUNK)

    def sums(c, carry):
        x = x_ref[rows_of(c), :]
        ss_ref[rows_of(c), :] = jnp.broadcast_to(jnp.sum(x * x, axis=-1, keepdims=True), (NORM_CHUNK, LANES))
        return carry

    lax.fori_loop(0, n, sums, 0, unroll=4)

    def apply(c, carry):
        rs = lax.rsqrt(ss_ref[rows_of(c), :] * (1.0 / D_MODEL) + RMS_EPS)
        y = x_ref[rows_of(c), :] * jnp.concatenate([rs] * (D_MODEL // LANES), axis=1)
        h_ref[rows_of(c), :] = ((y * g) * (1.0 + scale) + shift).astype(BF16)
        return carry

    lax.fori_loop(0, n, apply, 0, unroll=2)


def _mod_kernel(c_ref, w_ref, b_ref, o_ref):
    a = _silu(c_ref[...]).astype(BF16)
    o_ref[0] = jnp.dot(a, w_ref[0].astype(BF16), preferred_element_type=F32) + b_ref[0]


def _modulation(c_all, w_mod, b_mod):
    tn = 1024
    n6 = 6 * D_MODEL
    return pl.pallas_call(
        _mod_kernel,
        grid=(DEPTH, n6 // tn),
        in_specs=[
            pl.BlockSpec((SUBLANES, D_MODEL), lambda l, j: (0, 0)),
            pl.BlockSpec((1, D_MODEL, tn), lambda l, j: (l, 0, j)),
            pl.BlockSpec((1, 1, tn), lambda l, j: (l, 0, j)),
        ],
        out_specs=pl.BlockSpec((1, SUBLANES, tn), lambda l, j: (l, 0, j)),
        out_shape=jax.ShapeDtypeStruct((DEPTH, SUBLANES, n6), F32),
        compiler_params=_cparams(("arbitrary", "arbitrary")),
        name="modulation",
    )(c_all, w_mod, b_mod.reshape(DEPTH, 1, n6))


def _inproj_kernel(x_ref, mod_ref, g_ref, w_ref, o_ref, h_ref, ss_ref):
    @pl.when(pl.program_id(1) == 0)
    def _():
        m = mod_ref[0]
        _modnorm_store(x_ref, h_ref, ss_ref, g_ref[...], m[1:2], m[0:1])

    o_ref[...] = jnp.dot(h_ref[...], w_ref[0], preferred_element_type=F32)


def _inproj(x, mod_l, g, w, l):
    tm, tn = 1024, 512
    per_group = GROUP_ROWS // tm
    return pl.pallas_call(
        _inproj_kernel,
        grid=(N_ROWS // tm, P_PAD // tn),
        in_specs=[
            pl.BlockSpec((tm, D_MODEL), lambda i, j: (i, 0)),
            pl.BlockSpec((1, 6, D_MODEL), lambda i, j: (i // per_group, 0, 0)),
            pl.BlockSpec((1, D_MODEL), lambda i, j: (0, 0)),
            pl.BlockSpec((1, D_MODEL, tn), lambda i, j: (l, 0, j)),
        ],
        out_specs=pl.BlockSpec((tm, tn), lambda i, j: (i, j)),
        out_shape=jax.ShapeDtypeStruct((N_ROWS, P_PAD), F32),
        scratch_shapes=[pltpu.VMEM((tm, D_MODEL), BF16), pltpu.VMEM((tm, LANES), F32)],
        compiler_params=_cparams(("arbitrary", "arbitrary")),
        name="inproj",
    )(x, mod_l, g, w)


PREP_TM = 256
HALO = GRID_W


def _shift(x, top, bot, mu, is_ctx, has_top, has_bot):
    tm = x.shape[0]
    assert tm % CTX_LEN == 0 and tm % GRID_W == 0
    period = jnp.where(is_ctx, CTX_LEN, GRID_W)
    pos = lax.broadcasted_iota(jnp.int32, (tm, 1), 0) & (period - 1)
    prev = jnp.where(pos == 0, 0.0, pltpu.roll(x, 1, axis=0))
    nxt = jnp.where(pos == period - 1, 0.0, pltpu.roll(x, tm - 1, axis=0))
    up = jnp.concatenate([jnp.where(has_top, top, 0.0), x[: tm - HALO]], axis=0)
    down = jnp.concatenate([x[HALO:], jnp.where(has_bot, bot, 0.0)], axis=0)
    mu_up = jnp.where(is_ctx, 0.0, mu[2:3])
    mu_down = jnp.where(is_ctx, 0.0, mu[3:4])
    centre = 1.0 - mu[0:1] - mu[1:2] - mu_up - mu_down
    return x * centre + mu[0:1] * prev + mu[1:2] * nxt + mu_up * up + mu_down * down


def _head_sum(x):
    s = x[:, 0:LANES]
    for j in range(1, D_A // LANES):
        s = s + x[:, j * LANES:(j + 1) * LANES]
    for sh in (N_HEADS, 2 * N_HEADS, 4 * N_HEADS):
        s = s + pltpu.roll(s, sh, axis=1)
    return s


def _tile_heads(s):
    return jnp.concatenate([s] * (D_A // LANES), axis=1)


def _prep_kernel(r_ref, rt_ref, rb_ref, k_ref, kt_ref, kb_ref, v_ref, vt_ref, vb_ref,
                 l_ref, lt_ref, lb_ref, mu_ref, w0_ref, a0_ref, w2_ref, a2_ref, g2_ref,
                 kk_ref, ka_ref, rk_ref,
                 nkk_o, dec0_o, dec1_o, b0_o, b1_o, kd0_o, kd1_o, r_o, v_o, gate_o, vnat_o, bsum_o,
                 *, ctx_tiles, tiles_per_image):
    i = pl.program_id(0)
    is_ctx = i < ctx_tiles
    img_tile = (i - ctx_tiles) % tiles_per_image
    has_top = img_tile != 0
    has_bot = img_tile != tiles_per_image - 1
    sh = functools.partial(_shift, is_ctx=is_ctx, has_top=has_top, has_bot=has_bot)

    r = sh(r_ref[...], rt_ref[...], rb_ref[...], mu_ref[:, COL_R:COL_R + D_A])
    k = sh(k_ref[...], kt_ref[...], kb_ref[...], mu_ref[:, COL_K:COL_K + D_A])
    v = sh(v_ref[...], vt_ref[...], vb_ref[...], mu_ref[:, COL_V:COL_V + D_A])
    lo = sh(l_ref[...], lt_ref[...], lb_ref[...], mu_ref[:, COL_LORA:COL_LORA + LORA_W])
    wt = jnp.tanh(lo[:, 0:LANES]).astype(BF16)
    ad = lo[:, LANES:2 * LANES].astype(BF16)
    gs = jax.nn.sigmoid(lo[:, 2 * LANES:]).astype(BF16)

    kk = k * kk_ref[...]
    inv = lax.rsqrt(jnp.maximum(_head_sum(kk * kk), 1e-24))
    kk = kk * _tile_heads(inv)
    r_o[...] = r.T
    v_o[...] = v.T
    vnat_o[...] = v
    nkk_o[...] = (-kk).T
    gate_o[...] = jnp.dot(gs, g2_ref[...], preferred_element_type=F32)
    ka = ka_ref[...]
    rrk = r * rk_ref[...]
    bsum = None
    for d, (dec_o, b_o, kd_o) in enumerate(((dec0_o, b0_o, kd0_o), (dec1_o, b1_o, kd1_o))):
        wl = w0_ref[d:d + 1] + jnp.dot(wt, w2_ref[d], preferred_element_type=F32)
        dec_o[...] = jnp.exp(-(math.exp(-0.5) * jax.nn.sigmoid(wl))).T
        a = jax.nn.sigmoid(a0_ref[d:d + 1] + jnp.dot(ad, a2_ref[d], preferred_element_type=F32))
        kd = k * (1.0 + (a - 1.0) * ka)
        kd_o[...] = kd.T
        b_o[...] = (kk * a).T
        bonus = _head_sum(rrk * kd)
        bsum = bonus if bsum is None else bsum + bonus
    bsum_o[...] = bsum


def _prep(p, mu, w0, a0, w2, a2, g2, k_k, k_a, r_k, *, n_rows, ctx_rows, image_rows):
    tm = PREP_TM
    hb = tm // HALO
    n_halo = n_rows // HALO

    def main(col0, width):
        return pl.BlockSpec((tm, width), lambda i: (i, col0 // width))

    def top(col0, width):
        return pl.BlockSpec((HALO, width), lambda i: (jnp.maximum(i * hb - 1, 0), col0 // width))

    def bot(col0, width):
        return pl.BlockSpec((HALO, width), lambda i: (jnp.minimum(i * hb + hb, n_halo - 1), col0 // width))

    def full(a):
        nd = a.ndim
        return pl.BlockSpec(a.shape, lambda i: (0,) * nd)

    in_specs = []
    for col0 in (COL_R, COL_K, COL_V):
        in_specs += [main(col0, D_A), top(col0, D_A), bot(col0, D_A)]
    in_specs += [main(COL_LORA, LORA_W), top(COL_LORA, LORA_W), bot(COL_LORA, LORA_W)]
    params = (mu, w0, a0, w2, a2, g2, k_k, k_a, r_k)
    in_specs += [full(a) for a in params]
    t_spec = pl.BlockSpec((D_A, tm), lambda i: (0, i))
    n_spec = pl.BlockSpec((tm, D_A), lambda i: (i, 0))
    t_shape = jax.ShapeDtypeStruct((D_A, n_rows), F32)
    n_shape = jax.ShapeDtypeStruct((n_rows, D_A), F32)
    return pl.pallas_call(
        functools.partial(_prep_kernel, ctx_tiles=ctx_rows // tm, tiles_per_image=image_rows // tm),
        grid=(n_rows // tm,),
        in_specs=in_specs,
        out_specs=[t_spec] * 9 + [n_spec, n_spec, pl.BlockSpec((tm, LANES), lambda i: (i, 0))],
        out_shape=[t_shape] * 9 + [n_shape, n_shape, jax.ShapeDtypeStruct((n_rows, LANES), F32)],
        compiler_params=_cparams(("arbitrary",)),
        name="rwkv_prep",
    )(p, p, p, p, p, p, p, p, p, p, p, p, *params)


RELAYOUT_T = 256
FROM_CHAIN_T = 128
K_PER_BLOCK = LANES // N_HEADS


def _seq_col_block(stream, g, b, tb):
    row0, _, t_len = stream
    return (row0 + (g * SEQ_PER_GROUP + b) * t_len) // RELAYOUT_T + tb


def _to_chain_kernel(*refs, n_arrays):
    ins, outs = refs[:SEQ_PER_GROUP * n_arrays], refs[SEQ_PER_GROUP * n_arrays:]
    for a in range(n_arrays):
        blks = [ins[SEQ_PER_GROUP * a + b][...] for b in range(SEQ_PER_GROUP)]
        for kk in range(K_PER_BLOCK):
            slabs = [blk[kk * N_HEADS:(kk + 1) * N_HEADS] for blk in blks]
            outs[a][0, kk] = jnp.concatenate(slabs + slabs, axis=0).T


def _to_chain(arrays, stream):
    _, n_seq, t_len = stream
    groups = n_seq // SEQ_PER_GROUP
    n = len(arrays)
    in_specs, operands = [], []
    for a in arrays:
        for b in range(SEQ_PER_GROUP):
            in_specs.append(pl.BlockSpec(
                (LANES, RELAYOUT_T),
                functools.partial(lambda b, g, tb, kb: (kb, _seq_col_block(stream, g, b, tb)), b)))
            operands.append(a)
    out_spec = pl.BlockSpec((1, K_PER_BLOCK, RELAYOUT_T, LANES), lambda g, tb, kb: (g, kb, tb, 0))
    out_shape = jax.ShapeDtypeStruct((groups, HEAD, t_len, LANES), F32)
    return pl.pallas_call(
        functools.partial(_to_chain_kernel, n_arrays=n),
        grid=(groups, t_len // RELAYOUT_T, HEAD // K_PER_BLOCK),
        in_specs=in_specs,
        out_specs=[out_spec] * n,
        out_shape=[out_shape] * n,
        compiler_params=_cparams(("arbitrary", "arbitrary", "arbitrary")),
        name="to_chain",
    )(*operands)


def _value_to_chain_kernel(v0_ref, v1_ref, v2_ref, v3_ref, o_ref):
    blks = [r[...] for r in (v0_ref, v1_ref, v2_ref, v3_ref)]
    for vl in range(HALF):
        slabs = [blk[(vh * HALF + vl) * N_HEADS:(vh * HALF + vl + 1) * N_HEADS]
                 for vh in range(2) for blk in blks]
        o_ref[0, pl.ds(vl, RELAYOUT_T, stride=HALF), :] = jnp.concatenate(slabs, axis=0).T


def _value_to_chain(v_t, stream):
    _, n_seq, t_len = stream
    groups = n_seq // SEQ_PER_GROUP
    in_specs = [pl.BlockSpec((D_A, RELAYOUT_T),
                             functools.partial(lambda b, g, tb: (0, _seq_col_block(stream, g, b, tb)), b))
                for b in range(SEQ_PER_GROUP)]
    return pl.pallas_call(
        _value_to_chain_kernel,
        grid=(groups, t_len // RELAYOUT_T),
        in_specs=in_specs,
        out_specs=pl.BlockSpec((1, RELAYOUT_T * HALF, LANES), lambda g, tb: (g, tb, 0)),
        out_shape=jax.ShapeDtypeStruct((groups, t_len * HALF, LANES), F32),
        compiler_params=_cparams(("arbitrary", "arbitrary")),
        name="value_to_chain",
    )(v_t, v_t, v_t, v_t)


def _from_chain_kernel(z0_ref, z1_ref, o_ref, t_ref):
    b = pl.program_id(2)

    @pl.when(b == 0)
    def _():
        for vl in range(HALF):
            rows = pl.ds(vl, FROM_CHAIN_T, stride=HALF)
            t_ref[vl] = (z0_ref[0, rows, :] + z1_ref[0, rows, :]).T

    for vl in range(HALF):
        for vh in range(2):
            src = pl.ds(pl.multiple_of(vh * (LANES // 2) + b * N_HEADS, N_HEADS), N_HEADS)
            o_ref[(vh * HALF + vl) * N_HEADS:(vh * HALF + vl + 1) * N_HEADS, :] = t_ref[vl, src, :]


def _from_chain(z0, z1, stream, n_rows):
    _, n_seq, t_len = stream
    groups = n_seq // SEQ_PER_GROUP
    nt = t_len // FROM_CHAIN_T
    z_spec = pl.BlockSpec((1, FROM_CHAIN_T * HALF, LANES), lambda g, tb, b: (g, tb, 0))
    return pl.pallas_call(
        _from_chain_kernel,
        grid=(groups, nt, SEQ_PER_GROUP),
        in_specs=[z_spec, z_spec],
        out_specs=pl.BlockSpec((D_A, FROM_CHAIN_T), lambda g, tb, b: (0, (g * SEQ_PER_GROUP + b) * nt + tb)),
        out_shape=jax.ShapeDtypeStruct((D_A, n_seq * t_len), F32),
        scratch_shapes=[pltpu.VMEM((HALF, LANES, FROM_CHAIN_T), F32)],
        compiler_params=_cparams(("arbitrary", "arbitrary", "arbitrary")),
        name="from_chain",
    )(z0, z1)


SCAN_TT = 64
V_TILES = HALF // SUBLANES
ACC_PARTS = 2


def _scan_kernel(a_ref, w_ref, b_ref, kd_ref, r_ref, vv_ref, s0_ref, gnw_ref, gnb_ref,
                 z_ref, sfin_ref, s_ref, *, reverse):
    @pl.when(pl.program_id(1) == 0)
    def _():
        s_ref[...] = s0_ref[0]

    gnw = gnw_ref[...]
    gnb = gnb_ref[...]

    def block_row(i):
        return SCAN_TT - 1 - i if reverse else i

    def row(ref, k, ii):
        return ref[0, k, pl.ds(ii, 1), :]

    def value_rows(ii):
        start = ii * HALF
        return pl.ds(start if isinstance(start, int) else pl.multiple_of(start, HALF), HALF)

    def accumulate(parts, k, term):
        j = k % ACC_PARTS
        parts[j] = term if parts[j] is None else parts[j] + term

    def total(parts):
        out = parts[0]
        for p in parts[1:]:
            out = out + p
        return out

    def group_norm_store(y, ii):
        def head_mean(x):
            part = jnp.sum(jnp.sum(x, axis=0), axis=0, keepdims=True)
            return (part + pltpu.roll(part, LANES // 2, axis=1)) * (1.0 / HEAD)

        yc = y - head_mean(y)
        z = yc * lax.rsqrt(head_mean(yc * yc) + GN_EPS) * gnw + gnb
        z_ref[0, value_rows(ii), :] = z.reshape(HALF, LANES)

    first = block_row(0)
    parts = [None] * ACC_PARTS
    for k in range(HEAD):
        accumulate(parts, k, s_ref[k] * row(a_ref, k, first))

    def step(i, carry):
        sa, y_prev = carry
        ii = block_row(i)
        ii_next = block_row(jnp.minimum(i + 1, SCAN_TT - 1))
        group_norm_store(y_prev, block_row(jnp.maximum(i - 1, 0)))
        vv = vv_ref[0, value_rows(ii), :].reshape(V_TILES, SUBLANES, LANES)
        y_parts = [None] * ACC_PARTS
        sa_parts = [None] * ACC_PARTS
        for k in range(HEAD):
            s = s_ref[k] * row(w_ref, k, ii) + sa * row(b_ref, k, ii) + vv * row(kd_ref, k, ii)
            s_ref[k] = s
            accumulate(y_parts, k, s * row(r_ref, k, ii))
            accumulate(sa_parts, k, s * row(a_ref, k, ii_next))
        return total(sa_parts), total(y_parts)

    zero = jnp.zeros((V_TILES, SUBLANES, LANES), F32)
    _, y_last = lax.fori_loop(0, SCAN_TT, step, (total(parts), zero), unroll=4)
    group_norm_store(y_last, block_row(SCAN_TT - 1))

    @pl.when(pl.program_id(1) == pl.num_programs(1) - 1)
    def _():
        sfin_ref[0] = s_ref[...]


def _scan(a, w, b, kd, r, vv, s0, gnw, gnb, *, reverse):
    g, t = a.shape[0], a.shape[2]
    tt = SCAN_TT
    nt = t // tt
    tblk = (lambda ti: nt - 1 - ti) if reverse else (lambda ti: ti)
    kspec = pl.BlockSpec((1, HEAD, tt, LANES), lambda gi, ti: (gi, 0, tblk(ti), 0))
    vspec = pl.BlockSpec((1, tt * HALF, LANES), lambda gi, ti: (gi, tblk(ti), 0))
    sspec = pl.BlockSpec((1, HEAD, V_TILES, SUBLANES, LANES), lambda gi, ti: (gi, 0, 0, 0, 0))
    pspec = pl.BlockSpec((V_TILES, SUBLANES, LANES), lambda gi, ti: (0, 0, 0))
    return pl.pallas_call(
        functools.partial(_scan_kernel, reverse=reverse),
        grid=(g, nt),
        in_specs=[kspec] * 5 + [vspec, sspec, pspec, pspec],
        out_specs=[vspec, sspec],
        out_shape=[jax.ShapeDtypeStruct((g, t * HALF, LANES), F32),
                   jax.ShapeDtypeStruct((g, HEAD, V_TILES, SUBLANES, LANES), F32)],
        scratch_shapes=[pltpu.VMEM((HEAD, V_TILES, SUBLANES, LANES), F32)],
        compiler_params=_cparams(("arbitrary", "arbitrary")),
        name="delta_scan_bwd" if reverse else "delta_scan_fwd",
    )(a, w, b, kd, r, vv, s0, gnw, gnb)


def _gmlp_kernel(u_ref, v_ref, lng_ref, lnb_ref, ws_ref, bs_ref, o_ref):
    g = pl.program_id(1)
    v = _gelu_tanh(v_ref[...])
    mean = jnp.mean(v, axis=-1, keepdims=True)
    vc = v - mean
    var = jnp.mean(vc * vc, axis=-1, keepdims=True)
    vn = (vc * lax.rsqrt(var + LN_EPS) * lng_ref[pl.ds(g, 1), :] + lnb_ref[pl.ds(g, 1), :]).astype(BF16)
    u = _gelu_tanh(u_ref[...])
    ws = ws_ref[0]
    bs = bs_ref[0]
    for c in range(u.shape[0] // CHUNK):
        rows = slice(c * CHUNK, (c + 1) * CHUNK)
        s = jnp.dot(ws, vn[rows], preferred_element_type=F32) + bs
        o_ref[rows, :] = (u[rows] * s).astype(o_ref.dtype)


def _gmlp(p, ln_g, ln_b, w_sp, b_sp):
    tm = 2048
    ub, gb = COL_U // LANES, COL_G // LANES
    return pl.pallas_call(
        _gmlp_kernel,
        grid=(N_ROWS // tm, N_GROUPS_B),
        in_specs=[
            pl.BlockSpec((tm, LANES), lambda i, g: (i, ub + g)),
            pl.BlockSpec((tm, LANES), lambda i, g: (i, gb + g)),
            pl.BlockSpec((N_GROUPS_B, LANES), lambda i, g: (0, 0)),
            pl.BlockSpec((N_GROUPS_B, LANES), lambda i, g: (0, 0)),
            pl.BlockSpec((1, CHUNK, CHUNK), lambda i, g: (g, 0, 0)),
            pl.BlockSpec((1, CHUNK, LANES), lambda i, g: (g, 0, 0)),
        ],
        out_specs=pl.BlockSpec((tm, LANES), lambda i, g: (i, g)),
        out_shape=jax.ShapeDtypeStruct((N_ROWS, D_B), BF16),
        compiler_params=_cparams(("arbitrary", "arbitrary")),
        name="chunk_gmlp",
    )(p, p, ln_g, ln_b, w_sp, b_sp)


def _outproj_kernel(x_ref, zc_ref, zl_ref, v_ref, bsum_ref, gate_ref, ob_ref, mod_ref, w_ref, o_ref, a_ref,
                    *, ctx_tiles):
    i = pl.program_id(0)

    @pl.when(pl.program_id(1) == 0)
    def _():
        def fill(z_ref):
            oa = (z_ref[...].T + _tile_heads(bsum_ref[...]) * v_ref[...]) * gate_ref[...]
            a_ref[:, :D_A] = oa.astype(BF16)

        pl.when(i < ctx_tiles)(lambda: fill(zc_ref))
        pl.when(i >= ctx_tiles)(lambda: fill(zl_ref))
        a_ref[:, D_A:] = ob_ref[...]

    tn = o_ref.shape[1]
    j = pl.program_id(1)
    g1 = mod_ref[0, 2:3, pl.ds(pl.multiple_of(j * tn, tn), tn)]
    o_ref[...] = x_ref[...] + g1 * jnp.dot(a_ref[...], w_ref[0], preferred_element_type=F32)


def _outproj(x, z_ctx, z_lat, v, bsum, gate, ob, mod_l, w, l):
    tm, tn = 512, 1024
    per_group = GROUP_ROWS // tm
    ctx_tiles = CTX_ROWS // tm
    n_lat_tiles = (N_ROWS - CTX_ROWS) // tm
    half = pl.BlockSpec((tm, D_A), lambda i, j: (i, 0))
    return pl.pallas_call(
        functools.partial(_outproj_kernel, ctx_tiles=ctx_tiles),
        grid=(N_ROWS // tm, D_MODEL // tn),
        in_specs=[
            pl.BlockSpec((tm, tn), lambda i, j: (i, j)),
            pl.BlockSpec((D_A, tm), lambda i, j: (0, jnp.minimum(i, ctx_tiles - 1))),
            pl.BlockSpec((D_A, tm), lambda i, j: (0, jnp.clip(i - ctx_tiles, 0, n_lat_tiles - 1))),
            half,
            pl.BlockSpec((tm, LANES), lambda i, j: (i, 0)),
            half, half,
            pl.BlockSpec((1, 6, D_MODEL), lambda i, j: (i // per_group, 0, 0)),
            pl.BlockSpec((1, D_MODEL, tn), lambda i, j: (l, 0, j)),
        ],
        out_specs=pl.BlockSpec((tm, tn), lambda i, j: (i, j)),
        out_shape=jax.ShapeDtypeStruct((N_ROWS, D_MODEL), F32),
        scratch_shapes=[pltpu.VMEM((tm, D_MODEL), BF16)],
        compiler_params=_cparams(("arbitrary", "arbitrary")),
        name="outproj",
    )(x, z_ctx, z_lat, v, bsum, gate, ob, mod_l, w)


def _ffn_kernel(x_ref, mod_ref, g_ref, wg_ref, wu_ref, wd_ref, o_ref, h_ref, acc_ref, ss_ref):
    f = pl.program_id(1)

    @pl.when(f == 0)
    def _():
        m = mod_ref[0]
        _modnorm_store(x_ref, h_ref, ss_ref, g_ref[...], m[4:5], m[3:4])
        acc_ref[...] = jnp.zeros_like(acc_ref)

    h = h_ref[...]
    gate = jnp.dot(h, wg_ref[0], preferred_element_type=F32)
    up = jnp.dot(h, wu_ref[0], preferred_element_type=F32)
    act = (_silu(gate) * up).astype(BF16)
    acc_ref[...] += jnp.dot(act, wd_ref[0], preferred_element_type=F32)

    @pl.when(f == pl.num_programs(1) - 1)
    def _():
        o_ref[...] = x_ref[...] + mod_ref[0, 5:6] * acc_ref[...]


def _ffn(x, mod_l, g, wg, wu, wd, l):
    tm, tf = 512, 512
    per_group = GROUP_ROWS // tm
    return pl.pallas_call(
        _ffn_kernel,
        grid=(N_ROWS // tm, D_FF // tf),
        in_specs=[
            pl.BlockSpec((tm, D_MODEL), lambda i, f: (i, 0)),
            pl.BlockSpec((1, 6, D_MODEL), lambda i, f: (i // per_group, 0, 0)),
            pl.BlockSpec((1, D_MODEL), lambda i, f: (0, 0)),
            pl.BlockSpec((1, D_MODEL, tf), lambda i, f: (l, 0, f)),
            pl.BlockSpec((1, D_MODEL, tf), lambda i, f: (l, 0, f)),
            pl.BlockSpec((1, tf, D_MODEL), lambda i, f: (l, f, 0)),
        ],
        out_specs=pl.BlockSpec((tm, D_MODEL), lambda i, f: (i, 0)),
        out_shape=jax.ShapeDtypeStruct((N_ROWS, D_MODEL), F32),
        scratch_shapes=[pltpu.VMEM((tm, D_MODEL), BF16), pltpu.VMEM((tm, D_MODEL), F32),
                        pltpu.VMEM((tm, LANES), F32)],
        compiler_params=_cparams(("arbitrary", "arbitrary")),
        name="ffn",
    )(x, mod_l, g, wg, wu, wd)


def _final_norm_kernel(x_ref, g_ref, o_ref):
    x = x_ref[...]
    o_ref[...] = x * lax.rsqrt(jnp.mean(x * x, axis=-1, keepdims=True) + RMS_EPS) * g_ref[...]


def _final_norm(x, g, row0, rows):
    tm = 512
    return pl.pallas_call(
        _final_norm_kernel,
        grid=(rows // tm,),
        in_specs=[pl.BlockSpec((tm, D_MODEL), lambda i: (row0 // tm + i, 0)),
                  pl.BlockSpec((1, D_MODEL), lambda i: (0, 0))],
        out_specs=pl.BlockSpec((tm, D_MODEL), lambda i: (i, 0)),
        out_shape=jax.ShapeDtypeStruct((rows, D_MODEL), F32),
        compiler_params=_cparams(("arbitrary",)),
        name="final_norm",
    )(x, g)


def _head_param_tile(p):
    t = p.T.reshape(2, HALF, 1, N_HEADS)
    t = jnp.broadcast_to(t, (2, HALF, SEQ_PER_GROUP, N_HEADS))
    return t.transpose(1, 0, 2, 3).reshape(V_TILES, SUBLANES, LANES)


def _state_to_chain(s):
    g = s.shape[0]
    s = s.reshape(g, SEQ_PER_GROUP, N_HEADS, 2, HALF, HEAD)
    return s.transpose(0, 5, 4, 3, 1, 2).reshape(g, HEAD, V_TILES, SUBLANES, LANES)


def _state_from_chain(s):
    g = s.shape[0]
    s = s.reshape(g, HEAD, HALF, 2, SEQ_PER_GROUP, N_HEADS)
    return s.transpose(0, 4, 5, 3, 2, 1).reshape(g * SEQ_PER_GROUP, N_HEADS, HEAD, HEAD)


def _rwkv_stream(t_arrays, stream, s0_fwd, s0_bwd, gnw_t, gnb_t, n_rows):
    nkk_t, dec0_t, dec1_t, b0_t, b1_t, kd0_t, kd1_t, r_t, v_t = t_arrays
    nkk_c, dec0_c, dec1_c, b0_c, b1_c, kd0_c, kd1_c, r_c = _to_chain(
        (nkk_t, dec0_t, dec1_t, b0_t, b1_t, kd0_t, kd1_t, r_t), stream)
    vv_c = _value_to_chain(v_t, stream)
    z0, sf0 = _scan(nkk_c, dec0_c, b0_c, kd0_c, r_c, vv_c, s0_fwd, gnw_t, gnb_t, reverse=False)
    z1, sf1 = _scan(nkk_c, dec1_c, b1_c, kd1_c, r_c, vv_c, s0_bwd, gnw_t, gnb_t, reverse=True)
    return _from_chain(z0, z1, stream, n_rows), sf0, sf1


def _index_major(c, axis=-1):
    axis = axis % c.ndim
    split = c.reshape(c.shape[:axis] + (N_HEADS, HEAD) + c.shape[axis + 1:])
    return jnp.swapaxes(split, axis, axis + 1).reshape(c.shape)


def kernel(x_prompt, x_sample, state_rwkv, c, c_ctx, w_mod, b_mod, norm1_g, w_in, mu_shift, w0, w2, a0, a2,
           g2, k_k, k_a, r_k, gn_w, gn_b, gmlp_ln_g, gmlp_ln_b, w_spatial, b_spatial, w_out, norm2_g,
           w_ffn_gate, w_ffn_up, w_ffn_down, final_norm_g):
    x = jnp.concatenate([x_prompt.reshape(CTX_ROWS, D_MODEL), x_sample.reshape(N_LAT * LAT_LEN, D_MODEL)], axis=0)
    c_all = jnp.concatenate([c_ctx[None], c, jnp.zeros((SUBLANES - 1 - N_LAT, D_MODEL), F32)], axis=0)
    mod = _modulation(c_all, w_mod, b_mod).reshape(DEPTH, SUBLANES, 6, D_MODEL)

    def pad_cols(cols):
        c0 = 3 * D_A
        r, k, v = (_index_major(cols[..., j * D_A:(j + 1) * D_A]) for j in range(3))
        wd = cols[..., c0:c0 + DECAY_LORA]
        ad = cols[..., c0 + DECAY_LORA:c0 + DECAY_LORA + A_LORA]
        gd = cols[..., c0 + DECAY_LORA + A_LORA:c0 + DECAY_LORA + A_LORA + GATE_LORA]
        ug = cols[..., c0 + DECAY_LORA + A_LORA + GATE_LORA:]
        z = lambda n: jnp.zeros(cols.shape[:-1] + (n,), cols.dtype)
        return jnp.concatenate([r, k, v, wd, z(LANES - DECAY_LORA), ad, z(LANES - A_LORA),
                                gd, z(2 * LANES - GATE_LORA), ug], axis=-1)

    def pad_rows(w, rows):
        return jnp.pad(w, [(0, 0)] * (w.ndim - 2) + [(0, rows - w.shape[-2]), (0, 0)])

    w_in_p = pad_cols(w_in).astype(BF16)
    mu_p = pad_cols(mu_shift)[..., :COL_U]
    w0_p, a0_p = _index_major(w0), _index_major(a0)
    w2_p = pad_rows(_index_major(w2), LANES).astype(BF16)
    a2_p = pad_rows(_index_major(a2), LANES).astype(BF16)
    g2_p = pad_rows(_index_major(g2), 2 * LANES).astype(BF16)
    kk_p, ka_p = _index_major(k_k)[:, None], _index_major(k_a)[:, None]
    rk_p = _index_major(r_k.reshape(DEPTH, 1, D_A))
    w_out_p = jnp.concatenate([_index_major(w_out[:, :D_A], axis=1), w_out[:, D_A:]], axis=1).astype(BF16)
    wg_p, wu_p, wd_p = w_ffn_gate.astype(BF16), w_ffn_up.astype(BF16), w_ffn_down.astype(BF16)
    w_sp_p = w_spatial.astype(BF16)
    b_sp_p = jnp.broadcast_to(b_spatial[..., None], (DEPTH, N_GROUPS_B, CHUNK, LANES))

    ctx_groups = N_CTX_SEQ // SEQ_PER_GROUP
    s0_ctx = jnp.zeros((ctx_groups, HEAD, V_TILES, SUBLANES, LANES), F32)
    ctx_states = []
    for l in range(DEPTH):
        mod_l = mod[l]
        p = _inproj(x, mod_l, norm1_g[l][None], w_in_p, l)

        outs = _prep(p, mu_p[l], w0_p[l], a0_p[l], w2_p[l], a2_p[l], g2_p[l], kk_p[l], ka_p[l], rk_p[l],
                     n_rows=N_ROWS, ctx_rows=CTX_ROWS, image_rows=LAT_LEN)
        t_arrays, (gate, v_nat, bsum) = outs[:9], outs[9:]
        gnw_t, gnb_t = _head_param_tile(gn_w[l]), _head_param_tile(gn_b[l])

        z_ctx, sf0, sf1 = _rwkv_stream(t_arrays, CTX_STREAM, s0_ctx, s0_ctx, gnw_t, gnb_t, N_ROWS)
        s_lat = state_rwkv[:, l]
        z_lat, _, _ = _rwkv_stream(t_arrays, LAT_STREAM, _state_to_chain(s_lat[None, :, 0]),
                                   _state_to_chain(s_lat[None, :, 1]), gnw_t, gnb_t, N_ROWS)
        ctx_states.append(jnp.stack([_state_from_chain(sf0), _state_from_chain(sf1)], axis=1))

        ob = _gmlp(p, gmlp_ln_g[l], gmlp_ln_b[l], w_sp_p[l], b_sp_p[l])
        x = _outproj(x, z_ctx, z_lat, v_nat, bsum, gate, ob, mod_l, w_out_p, l)
        x = _ffn(x, mod_l, norm2_g[l][None], wg_p, wu_p, wd_p, l)

    y_prompt = _final_norm(x, final_norm_g[None], 0, CTX_ROWS).reshape(N_CTX_SEQ, CTX_LEN, D_MODEL)
    y_sample = _final_norm(x, final_norm_g[None], CTX_ROWS, N_ROWS - CTX_ROWS).reshape(N_LAT, LAT_LEN, D_MODEL)
    new_state = jnp.stack(ctx_states, axis=1)
    return (y_prompt, y_sample, new_state)
```
